```python
import math
import jax
import jax.numpy as jnp
from jax import lax
import numpy as np

D_MODEL = 4096
BATCH = 2
SEQ = 8192
DEPTH = 2

HEAD_DIM = 128
N_SB = 8
N_DIFF = 8
N_MLA = 8
N_FOX = 8
N_MIX_HEADS = N_SB + N_DIFF + N_MLA + N_FOX
MLA_Q_RANK = 768
MLA_KV_RANK = 512
MLA_NOPE = 128
MLA_ROPE = 64
MLA_V = HEAD_DIM
MIX_WIDTH = N_MIX_HEADS * HEAD_DIM
N_MEM = 256
N_XHEADS = 4
D_FF = 11008
CONV_WIDTH = 3
N_BUCKETS = 32
MAX_DISTANCE = 128
ROPE_THETA = 10000.0
Q_BLOCK = 128
EPS = 1e-6
POS_OFFSET_RANGE = 1024

IN_WIDTHS = ((N_SB * HEAD_DIM,) * 3 + (N_DIFF * HEAD_DIM,) * 3
             + (MLA_Q_RANK, MLA_KV_RANK, MLA_ROPE) + (N_FOX * HEAD_DIM,) * 3 + (N_FOX,))
IN_COLS = sum(IN_WIDTHS)
IN_SPLITS = tuple(sum(IN_WIDTHS[:i + 1]) for i in range(len(IN_WIDTHS) - 1))

kernel_name = 'hybrid_parallel_head_group_decoder'

F32 = jnp.float32


def rms_norm(x, g):
    xf = x.astype(F32)
    y = xf * lax.rsqrt(jnp.mean(xf * xf, axis=-1, keepdims=True) + EPS)
    return (y * g.astype(F32)).astype(x.dtype)


def rope_tables(pos):
    inv = ROPE_THETA ** (-jnp.arange(0, MLA_ROPE, 2, dtype=F32) / MLA_ROPE)
    ang = pos.astype(F32)[..., None] * inv
    return jnp.cos(ang), jnp.sin(ang)


def apply_rope(x, cos, sin):
    x1, x2 = jnp.split(x.astype(F32), 2, axis=-1)
    return jnp.concatenate([x1 * cos - x2 * sin, x2 * cos + x1 * sin], axis=-1).astype(x.dtype)


def t5_bucket(rel):
    n = jnp.maximum(rel, 0)
    max_exact = N_BUCKETS // 2
    nf = jnp.maximum(n, 1).astype(F32)
    large = max_exact + (jnp.log(nf / max_exact) / math.log(MAX_DISTANCE / max_exact)
                         * (N_BUCKETS - max_exact)).astype(jnp.int32)
    large = jnp.minimum(large, N_BUCKETS - 1)
    return jnp.where(n < max_exact, n, large)


def sweep_query_blocks(block_fn, *q_side):
    S = q_side[0].shape[1]

    def body(i):
        start = i * Q_BLOCK
        sl = [lax.dynamic_slice_in_dim(a, start, Q_BLOCK, axis=1) for a in q_side]
        return block_fn(start + jnp.arange(Q_BLOCK), *sl)

    out = lax.map(body, jnp.arange(S // Q_BLOCK))
    nb, B, Q, H, dv = out.shape
    return jnp.moveaxis(out, 0, 1).reshape(B, nb * Q, H, dv)


def stick_breaking_attention(q, k, v):
    S, d = q.shape[1], q.shape[-1]
    k_idx = jnp.arange(S)
    scale = d ** -0.5

    def block(q_idx, qb):
        z = jnp.einsum('bqhd,bshd->bhqs', qb, k, preferred_element_type=F32) * scale
        strict = k_idx[None, :] < q_idx[:, None]
        log_keep = jnp.where(strict, -jax.nn.softplus(z), 0.0)
        between = lax.cumsum(log_keep, axis=3, reverse=True) - log_keep
        a = jnp.where(strict, jnp.exp(jax.nn.log_sigmoid(z) + between), 0.0)
        return jnp.einsum('bhqs,bshd->bqhd', a.astype(v.dtype), v)

    return sweep_query_blocks(block, q)


def differential_attention(q, k, v, pos, t5_table, lam):
    S, half = q.shape[1], q.shape[-1] // 2
    scale = half ** -0.5
    k1, k2 = k[..., :half], k[..., half:]
    k_idx = jnp.arange(S)
    table = t5_table.astype(F32)

    def block(q_idx, qb, pb):
        causal = k_idx[None, :] <= q_idx[:, None]
        bucket = t5_bucket(pb[:, :, None] - pos[:, None, :])
        bias = jnp.moveaxis(table[bucket], -1, 1)

        def probs(qh, kh):
            s = jnp.einsum('bqhd,bshd->bhqs', qh, kh, preferred_element_type=F32) * scale + bias
            return jax.nn.softmax(jnp.where(causal, s, -jnp.inf), axis=-1)

        a = probs(qb[..., :half], k1) - lam * probs(qb[..., half:], k2)
        return jnp.einsum('bhqs,bshd->bqhd', a.astype(v.dtype), v)

    return sweep_query_blocks(block, q, pos)


def mla_attention(q_nope, q_rope, k_nope, k_rope, v):
    S = q_nope.shape[1]
    scale = (MLA_NOPE + MLA_ROPE) ** -0.5
    k_idx = jnp.arange(S)

    def block(q_idx, qn, qr):
        causal = k_idx[None, :] <= q_idx[:, None]
        s = (jnp.einsum('bqhd,bshd->bhqs', qn, k_nope, preferred_element_type=F32)
             + jnp.einsum('bqhr,bsr->bhqs', qr, k_rope, preferred_element_type=F32)) * scale
        p = jax.nn.softmax(jnp.where(causal, s, -jnp.inf), axis=-1)
        return jnp.einsum('bhqs,bshd->bqhd', p.astype(v.dtype), v)

    return sweep_query_blocks(block, q_nope, q_rope)


def forgetting_attention(q, k, v, log_f):
    S, d = q.shape[1], q.shape[-1]
    scale = d ** -0.5
    k_idx = jnp.arange(S)
    c = lax.cumsum(log_f.astype(F32), axis=1)
    c_k = jnp.moveaxis(c, -1, 1)[:, :, None, :]

    def block(q_idx, qb, cb):
        causal = k_idx[None, :] <= q_idx[:, None]
        decay = jnp.moveaxis(cb, -1, 1)[..., None] - c_k
        s = jnp.einsum('bqhd,bshd->bhqs', qb, k, preferred_element_type=F32) * scale + decay
        p = jax.nn.softmax(jnp.where(causal, s, -jnp.inf), axis=-1)
        return jnp.einsum('bhqs,bshd->bqhd', p.astype(v.dtype), v)

    return sweep_query_blocks(block, q, c)


def hybrid_mixer(h, pos, cos, sin, t5_table, w_in, q_norm, w_uq, kv_norm, w_ukv,
                 lam_vecs, b_forget, head_norm, w_out, layer_idx):
    B, S, _ = h.shape
    proj = h @ w_in
    (sb_q, sb_k, sb_v, df_q, df_k, df_v, c_q, c_kv, k_r,
     fx_q, fx_k, fx_v, f_logit) = jnp.split(proj, IN_SPLITS, axis=-1)

    def heads(t, n):
        return t.reshape(B, S, n, -1)

    o_sb = stick_breaking_attention(heads(sb_q, N_SB), heads(sb_k, N_SB), heads(sb_v, N_SB))

    lv = lam_vecs.astype(F32)
    lam_init = 0.8 - 0.6 * math.exp(-0.3 * layer_idx)
    lam = jnp.exp(jnp.sum(lv[0] * lv[1])) - jnp.exp(jnp.sum(lv[2] * lv[3])) + lam_init
    o_df = differential_attention(heads(df_q, N_DIFF), heads(df_k, N_DIFF), heads(df_v, N_DIFF),
                                  pos, t5_table, lam)

    q_c = (rms_norm(c_q, q_norm) @ w_uq).reshape(B, S, N_MLA, MLA_NOPE + MLA_ROPE)
    q_nope = q_c[..., :MLA_NOPE]
    q_rope = apply_rope(q_c[..., MLA_NOPE:], cos[:, :, None, :], sin[:, :, None, :])
    kv = (rms_norm(c_kv, kv_norm) @ w_ukv).reshape(B, S, N_MLA, MLA_NOPE + MLA_V)
    k_nope, v_c = kv[..., :MLA_NOPE], kv[..., MLA_NOPE:]
    k_rope = apply_rope(k_r, cos, sin)
    o_mla = mla_attention(q_nope, q_rope, k_nope, k_rope, v_c)

    log_f = jax.nn.log_sigmoid(f_logit.astype(F32) + b_forget.astype(F32))
    o_fx = forgetting_attention(heads(fx_q, N_FOX), heads(fx_k, N_FOX), heads(fx_v, N_FOX), log_f)

    g_sb, g_df, g_mla, g_fx = jnp.split(head_norm, [N_SB, N_SB + N_DIFF, N_SB + N_DIFF + N_MLA], axis=0)
    o = jnp.concatenate([rms_norm(o_sb, g_sb),
                         rms_norm(o_df, g_df) * (1.0 - lam_init),
                         rms_norm(o_mla, g_mla),
                         rms_norm(o_fx, g_fx)], axis=2)
    return o.reshape(B, S, MIX_WIDTH) @ w_out


def memory_cross_attention(h, mem_n, w_q, w_k, w_v, w_o):
    B, S, _ = h.shape
    q = (h @ w_q).reshape(B, S, N_XHEADS, HEAD_DIM)
    k = (mem_n @ w_k).reshape(B, -1, N_XHEADS, HEAD_DIM)
    v = (mem_n @ w_v).reshape(B, -1, N_XHEADS, HEAD_DIM)
    s = jnp.einsum('bqhd,bmhd->bhqm', q, k, preferred_element_type=F32) * HEAD_DIM ** -0.5
    p = jax.nn.softmax(s, axis=-1)
    o = jnp.einsum('bhqm,bmhd->bqhd', p.astype(v.dtype), v).reshape(B, S, N_XHEADS * HEAD_DIM)
    return o @ w_o


def conv_glu_ffn(h, w_gate, w_up, conv_w, conv_b, w_down):
    gate = h @ w_gate
    taps = conv_w.astype(gate.dtype)[:, None, :]
    gate = lax.conv_general_dilated(gate, taps, window_strides=(1,), padding=[(CONV_WIDTH - 1, 0)],
                                    dimension_numbers=('NWC', 'WIO', 'NWC'),
                                    feature_group_count=gate.shape[-1]) + conv_b.astype(gate.dtype)
    act = jax.nn.gelu(gate, approximate=True) * (h @ w_up)
    return act @ w_down


def setup_inputs(seed: int = 0) -> dict:
    key = jax.random.key(seed)
    ks = iter(jax.random.split(key, 40))

    def nrm(shape, scale):
        return jax.random.normal(next(ks), shape, F32) * scale

    def gain(shape):
        return 1.0 + nrm(shape, 0.05)

    L, D, F = DEPTH, D_MODEL, D_FF
    XW = N_XHEADS * HEAD_DIM
    x = nrm((BATCH, SEQ, D), 1.0)
    mem = nrm((BATCH, N_MEM, D), 1.0)
    positions = (jnp.arange(SEQ, dtype=jnp.int32)[None, :]
                 + jax.random.randint(next(ks), (BATCH, 1), 0, POS_OFFSET_RANGE, dtype=jnp.int32))
    return {
        'x': x,
        'mem': mem,
        'positions': positions,
        't5_table': nrm((N_BUCKETS, N_DIFF), 0.5),
        'mix_norm_pre': gain((L, D)),
        'w_in': nrm((L, D, IN_COLS), D ** -0.5),
        'mla_q_norm': gain((L, MLA_Q_RANK)),
        'w_uq': nrm((L, MLA_Q_RANK, N_MLA * (MLA_NOPE + MLA_ROPE)), MLA_Q_RANK ** -0.5),
        'mla_kv_norm': gain((L, MLA_KV_RANK)),
        'w_ukv': nrm((L, MLA_KV_RANK, N_MLA * (MLA_NOPE + MLA_V)), MLA_KV_RANK ** -0.5),
        'diff_lambda': nrm((L, 4, HEAD_DIM // 2), 0.1),
        'b_forget': 2.0 + nrm((L, N_FOX), 0.5),
        'head_norm': gain((L, N_MIX_HEADS, HEAD_DIM)),
        'w_out': nrm((L, MIX_WIDTH, D), MIX_WIDTH ** -0.5),
        'mix_norm_post': gain((L, D)),
        'xattn_norm_pre': gain((L, D)),
        'mem_norm': gain((L, D)),
        'w_xq': nrm((L, D, XW), D ** -0.5),
        'w_xk': nrm((L, D, XW), D ** -0.5),
        'w_xv': nrm((L, D, XW), D ** -0.5),
        'w_xo': nrm((L, XW, D), XW ** -0.5),
        'xattn_norm_post': gain((L, D)),
        'ffn_norm_pre': gain((L, D)),
        'w_gate': nrm((L, D, F), D ** -0.5),
        'w_up': nrm((L, D, F), D ** -0.5),
        'conv_w': nrm((L, CONV_WIDTH, F), CONV_WIDTH ** -0.5),
        'conv_b': nrm((L, F), 0.01),
        'w_down': nrm((L, F, D), F ** -0.5),
        'ffn_norm_post': gain((L, D)),
    }


def reference(x, mem, positions, t5_table, mix_norm_pre, w_in, mla_q_norm, w_uq, mla_kv_norm, w_ukv,
              diff_lambda, b_forget, head_norm, w_out, mix_norm_post, xattn_norm_pre, mem_norm,
              w_xq, w_xk, w_xv, w_xo, xattn_norm_post, ffn_norm_pre, w_gate, w_up, conv_w, conv_b,
              w_down, ffn_norm_post):
    cos, sin = rope_tables(positions)
    for l in range(DEPTH):
        h = rms_norm(x, mix_norm_pre[l])
        y = hybrid_mixer(h, positions, cos, sin, t5_table, w_in[l], mla_q_norm[l], w_uq[l],
                         mla_kv_norm[l], w_ukv[l], diff_lambda[l], b_forget[l], head_norm[l],
                         w_out[l], l)
        x = x + rms_norm(y, mix_norm_post[l])
        h = rms_norm(x, xattn_norm_pre[l])
        y = memory_cross_attention(h, rms_norm(mem, mem_norm[l]), w_xq[l], w_xk[l], w_xv[l], w_xo[l])
        x = x + rms_norm(y, xattn_norm_post[l])
        h = rms_norm(x, ffn_norm_pre[l])
        y = conv_glu_ffn(h, w_gate[l], w_up[l], conv_w[l], conv_b[l], w_down[l])
        x = x + rms_norm(y, ffn_norm_post[l])
    return x
```

```python
import functools
import math

import jax
import jax.numpy as jnp
from jax import lax
from jax.experimental import pallas as pl
from jax.experimental.pallas import tpu as pltpu

F32 = jnp.float32
BF16 = jnp.bfloat16

HEAD_DIM = 128
N_SB = 8
N_DIFF = 8
N_MLA = 8
N_FOX = 8
MLA_Q_RANK = 768
MLA_KV_RANK = 512
MLA_NOPE = 128
MLA_ROPE = 64
MLA_QK_PAD = 256
N_XHEADS = 4
N_BUCKETS = 32
MAX_DISTANCE = 128
ROPE_THETA = 10000.0
EPS = 1e-6
NEG_BIG = -1e30

V7X_VMEM_LIMIT_BYTES = 56 * 1024 * 1024
ATTN_BLOCK = 512
FFN_PAD = 1024


def _params(*sem):
    return pltpu.CompilerParams(dimension_semantics=sem, vmem_limit_bytes=V7X_VMEM_LIMIT_BYTES)


def _rms(x, g):
    return x * lax.rsqrt(jnp.mean(x * x, axis=-1, keepdims=True) + EPS) * g


def _rmsnorm_kernel(x_ref, g_ref, o_ref):
    o_ref[...] = _rms(x_ref[...].astype(F32), g_ref[...]).astype(o_ref.dtype)


def rmsnorm(x, g, out_dtype=BF16, bm=512):
    m, d = x.shape
    bm = min(bm, m)
    return pl.pallas_call(
        _rmsnorm_kernel,
        out_shape=jax.ShapeDtypeStruct((m, d), out_dtype),
        grid=(pl.cdiv(m, bm),),
        in_specs=[pl.BlockSpec((bm, d), lambda i: (i, 0)), pl.BlockSpec((1, d), lambda i: (0, 0))],
        out_specs=pl.BlockSpec((bm, d), lambda i: (i, 0)),
        compiler_params=_params("parallel"),
        name="rmsnorm",
    )(x, g.reshape(1, d).astype(F32))


def _addnorm_kernel(y_ref, x_ref, gpost_ref, gnext_ref, xo_ref, ho_ref):
    xn = x_ref[...] + _rms(y_ref[...], gpost_ref[...])
    xo_ref[...] = xn
    ho_ref[...] = _rms(xn, gnext_ref[...]).astype(ho_ref.dtype)


def _addnorm_last_kernel(y_ref, x_ref, gpost_ref, xo_ref):
    xo_ref[...] = x_ref[...] + _rms(y_ref[...], gpost_ref[...])


def addnorm(y, x, g_post, g_next=None, bm=256):
    m, d = x.shape
    bm = min(bm, m)
    row = pl.BlockSpec((bm, d), lambda i: (i, 0))
    vec = pl.BlockSpec((1, d), lambda i: (0, 0))
    if g_next is None:
        return pl.pallas_call(
            _addnorm_last_kernel,
            out_shape=jax.ShapeDtypeStruct((m, d), F32),
            grid=(pl.cdiv(m, bm),), in_specs=[row, row, vec], out_specs=row,
            compiler_params=_params("parallel"), name="addnorm_last",
        )(y, x, g_post.reshape(1, d)), None
    return pl.pallas_call(
        _addnorm_kernel,
        out_shape=(jax.ShapeDtypeStruct((m, d), F32), jax.ShapeDtypeStruct((m, d), BF16)),
        grid=(pl.cdiv(m, bm),), in_specs=[row, row, vec, vec], out_specs=(row, row),
        compiler_params=_params("parallel"), name="addnorm",
    )(y, x, g_post.reshape(1, d), g_next.reshape(1, d))


def _matmul_kernel(a_ref, b_ref, o_ref):
    o_ref[...] = jnp.dot(a_ref[...], b_ref[...], preferred_element_type=F32).astype(o_ref.dtype)


def _matmul_acc_kernel(a_ref, b_ref, o_ref, acc_ref, *, nk):
    k = pl.program_id(2)

    @pl.when(k == 0)
    def _():
        acc_ref[...] = jnp.zeros_like(acc_ref)

    acc_ref[...] += jnp.dot(a_ref[...], b_ref[...], preferred_element_type=F32)

    @pl.when(k == nk - 1)
    def _():
        o_ref[...] = acc_ref[...].astype(o_ref.dtype)


def _pick_block(n, target):
    if n <= target:
        return n
    best = 128
    for cand in range(128, target + 1, 128):
        if n % cand == 0:
            best = cand
    assert n % best == 0
    return best


def matmul(a, b, out_dtype, bm=1024, bn=1024, bk=None):
    m, kdim = a.shape
    n = b.shape[1]
    bm, bn = _pick_block(m, bm), _pick_block(n, bn)
    bk = kdim if bk is None else bk
    assert kdim % bk == 0
    nk = kdim // bk
    if nk == 1:
        return pl.pallas_call(
            _matmul_kernel,
            out_shape=jax.ShapeDtypeStruct((m, n), out_dtype),
            grid=(pl.cdiv(m, bm), pl.cdiv(n, bn)),
            in_specs=[pl.BlockSpec((bm, kdim), lambda i, j: (i, 0)),
                      pl.BlockSpec((kdim, bn), lambda i, j: (0, j))],
            out_specs=pl.BlockSpec((bm, bn), lambda i, j: (i, j)),
            compiler_params=_params("parallel", "parallel"), name="matmul",
        )(a, b)
    return pl.pallas_call(
        functools.partial(_matmul_acc_kernel, nk=nk),
        out_shape=jax.ShapeDtypeStruct((m, n), out_dtype),
        grid=(pl.cdiv(m, bm), pl.cdiv(n, bn), nk),
        in_specs=[pl.BlockSpec((bm, bk), lambda i, j, k: (i, k)),
                  pl.BlockSpec((bk, bn), lambda i, j, k: (k, j))],
        out_specs=pl.BlockSpec((bm, bn), lambda i, j, k: (i, j)),
        scratch_shapes=[pltpu.VMEM((bm, bn), F32)],
        compiler_params=_params("parallel", "parallel", "arbitrary"), name="matmul_acc",
    )(a, b)


def _ffn_up_kernel(h_ref, wg_ref, wu_ref, cw_ref, cb_ref, o_ref, carry_ref, *, tiles_per_seq):
    i = pl.program_id(1)
    h = h_ref[...]
    gate = jnp.dot(h, wg_ref[...], preferred_element_type=F32)
    up = jnp.dot(h, wu_ref[...], preferred_element_type=F32)
    bm = gate.shape[0]

    @pl.when(i % tiles_per_seq == 0)
    def _():
        carry_ref[...] = jnp.zeros_like(carry_ref)

    prev = carry_ref[...]
    carry_ref[...] = gate[bm - 8:, :]
    g1 = pltpu.roll(gate, 1, axis=0)
    g2 = pltpu.roll(gate, 2, axis=0)
    row8 = lax.broadcasted_iota(jnp.int32, (8, 1), 0)
    p1 = pltpu.roll(prev, 1, axis=0)
    p2 = pltpu.roll(prev, 2, axis=0)
    g1 = jnp.concatenate([jnp.where(row8 < 1, p1, g1[:8]), g1[8:]], axis=0)
    g2 = jnp.concatenate([jnp.where(row8 < 2, p2, g2[:8]), g2[8:]], axis=0)
    cw = cw_ref[...]
    conv = cw[0:1] * g2 + cw[1:2] * g1 + cw[2:3] * gate + cb_ref[...]
    o_ref[...] = (jax.nn.gelu(conv, approximate=True) * up).astype(o_ref.dtype)


def ffn_up(h, wg, wu, conv_w, conv_b, seq, bm=1024, bn=512):
    m, d = h.shape
    f = wg.shape[1]
    bm = _pick_block(seq, bm)
    bn = _pick_block(f, bn)
    assert seq % bm == 0 and m % seq == 0 and bm % 8 == 0
    return pl.pallas_call(
        functools.partial(_ffn_up_kernel, tiles_per_seq=seq // bm),
        out_shape=jax.ShapeDtypeStruct((m, f), BF16),
        grid=(pl.cdiv(f, bn), m // bm),
        in_specs=[pl.BlockSpec((bm, d), lambda j, i: (i, 0)),
                  pl.BlockSpec((d, bn), lambda j, i: (0, j)),
                  pl.BlockSpec((d, bn), lambda j, i: (0, j)),
                  pl.BlockSpec((3, bn), lambda j, i: (0, j)),
                  pl.BlockSpec((1, bn), lambda j, i: (0, j))],
        out_specs=pl.BlockSpec((bm, bn), lambda j, i: (i, j)),
        scratch_shapes=[pltpu.VMEM((8, bn), F32)],
        compiler_params=_params("arbitrary", "arbitrary"), name="ffn_up",
    )(h, wg, wu, conv_w.astype(F32), conv_b.reshape(1, f).astype(F32))


def _qk(q, k):
    return lax.dot_general(q, k, (((1,), (1,)), ((), ())), preferred_element_type=F32)


def _softmax_step(s, v, m_ref, l_ref, acc_ref):
    m_prev = m_ref[...]
    m_new = jnp.maximum(m_prev, jnp.max(s, axis=-1, keepdims=True))
    alpha = jnp.exp(m_prev - m_new)
    p = jnp.exp(s - m_new)
    l_ref[...] = alpha * l_ref[...] + jnp.sum(p, axis=-1, keepdims=True)
    acc_ref[...] = alpha * acc_ref[...] + jnp.dot(p.astype(BF16), v, preferred_element_type=F32)
    m_ref[...] = m_new


def _kv_block(k_ref, v_ref, j, bk):
    start = pl.multiple_of(j * bk, bk)
    return k_ref[0, pl.ds(start, bk), :], v_ref[0, pl.ds(start, bk), :]


def _local_mask(bq, bk, strict):
    row = lax.broadcasted_iota(jnp.int32, (bq, bk), 0)
    col = lax.broadcasted_iota(jnp.int32, (bq, bk), 1)
    return col < row if strict else col <= row


def _softmax_attn_kernel(*refs, fox, bq, bk):
    if fox:
        q_ref, k_ref, v_ref, g_ref, ccol_ref, crow_ref, o_ref, m_ref, l_ref, acc_ref = refs
    else:
        q_ref, k_ref, v_ref, g_ref, o_ref, m_ref, l_ref, acc_ref = refs
    i = pl.program_id(2)
    q = q_ref[0]
    m_ref[...] = jnp.full_like(m_ref, NEG_BIG)
    l_ref[...] = jnp.zeros_like(l_ref)
    acc_ref[...] = jnp.zeros_like(acc_ref)

    def block(j, masked):
        k, v = _kv_block(k_ref, v_ref, j, bk)
        s = _qk(q, k)
        if fox:
            s = s + (ccol_ref[0, 0] - crow_ref[0, 0, pl.ds(j, 1), :])
        if masked:
            s = jnp.where(_local_mask(bq, bk, False), s, NEG_BIG)
        _softmax_step(s, v, m_ref, l_ref, acc_ref)

    def body(j, c):
        block(j, False)
        return c

    lax.fori_loop(0, i, body, 0)
    block(i, True)
    o = acc_ref[...] / l_ref[...]
    o_ref[0] = _rms(o, g_ref[0]).astype(o_ref.dtype)


def _diff_attn_kernel(q_ref, k_ref, v_ref, g_ref, bdiag_ref, bsub_ref, far_ref, lam_ref, o_ref,
                      m1_ref, l1_ref, a1_ref, m2_ref, l2_ref, a2_ref, *, bq, bk, out_scale):
    i = pl.program_id(2)
    q = q_ref[0]
    half = lax.broadcasted_iota(jnp.int32, q.shape, 1) < (HEAD_DIM // 2)
    zero = jnp.zeros_like(q)
    q1 = jnp.where(half, q, zero)
    q2 = jnp.where(half, zero, q)
    for m_ref, l_ref, a_ref in ((m1_ref, l1_ref, a1_ref), (m2_ref, l2_ref, a2_ref)):
        m_ref[...] = jnp.full_like(m_ref, NEG_BIG)
        l_ref[...] = jnp.zeros_like(l_ref)
        a_ref[...] = jnp.zeros_like(a_ref)

    def block(j, bias, masked):
        k, v = _kv_block(k_ref, v_ref, j, bk)
        s1 = _qk(q1, k)
        s2 = _qk(q2, k)
        if bias is not None:
            s1 = s1 + bias
            s2 = s2 + bias
        if masked:
            mask = _local_mask(bq, bk, False)
            s1 = jnp.where(mask, s1, NEG_BIG)
            s2 = jnp.where(mask, s2, NEG_BIG)
        _softmax_step(s1, v, m1_ref, l1_ref, a1_ref)
        _softmax_step(s2, v, m2_ref, l2_ref, a2_ref)

    far = far_ref[0][:, :1]

    def body(j, c):
        block(j, None, False)
        return c

    lax.fori_loop(0, i - 1, body, 0)
    m1_ref[...] = m1_ref[...] + far
    m2_ref[...] = m2_ref[...] + far

    @pl.when(i >= 1)
    def _():
        block(i - 1, bsub_ref[0], False)

    block(i, bdiag_ref[0], True)
    lam = lam_ref[...][:, :1]
    o = a1_ref[...] / l1_ref[...] - lam * (a2_ref[...] / l2_ref[...])
    o_ref[0] = (_rms(o, g_ref[0]) * out_scale).astype(o_ref.dtype)


def _sb_attn_kernel(q_ref, k_ref, v_ref, g_ref, u_ref, o_ref, r_ref, acc_ref, *, bq, bk):
    i = pl.program_id(2)
    q = q_ref[0]
    r_ref[...] = jnp.zeros_like(r_ref)
    acc_ref[...] = jnp.zeros_like(acc_ref)

    def block(j, masked):
        k, v = _kv_block(k_ref, v_ref, j, bk)
        z = _qk(q, k)
        sp = jnp.maximum(z, 0.0) + jnp.log(1.0 + jnp.exp(-jnp.abs(z)))
        log_keep = -sp
        if masked:
            strict = _local_mask(bq, bk, True)
            log_keep = jnp.where(strict, log_keep, 0.0)
        hi = log_keep.astype(BF16)
        lo = (log_keep - hi.astype(F32)).astype(BF16)
        u = u_ref[...]
        between = (jnp.dot(hi, u, preferred_element_type=F32)
                   + jnp.dot(lo, u, preferred_element_type=F32))
        r = r_ref[...]
        a = jnp.exp((z - sp) + between + r)
        if masked:
            a = jnp.where(strict, a, 0.0)
        acc_ref[...] += jnp.dot(a.astype(BF16), v, preferred_element_type=F32)
        r_ref[...] = r + jnp.sum(log_keep, axis=-1, keepdims=True)

    block(i, True)

    def body(t, c):
        block(i - 1 - t, False)
        return c

    lax.fori_loop(0, i, body, 0)
    o_ref[0] = _rms(acc_ref[...], g_ref[0]).astype(o_ref.dtype)


def _attn_specs(bq, seq, dq, dk, q_off, k_off, v_off):
    return [pl.BlockSpec((1, bq, dq), lambda b, h, i: (b, i, q_off + h)),
            pl.BlockSpec((1, seq, dk), lambda b, h, i: (b, 0, k_off + h)),
            pl.BlockSpec((1, seq, HEAD_DIM), lambda b, h, i: (b, 0, v_off + h)),
            pl.BlockSpec((1, 1, HEAD_DIM), lambda b, h, i: (h, 0, 0))]


def _attn_call(kernel, n_heads, q_arr, k_arr, v_arr, gain, extra_in, extra_specs, scratch,
               dq, dk, q_off, k_off, v_off, name):
    nb, seq, _ = q_arr.shape
    bq = min(ATTN_BLOCK, seq)
    assert seq % bq == 0
    return pl.pallas_call(
        functools.partial(kernel, bq=bq, bk=bq),
        out_shape=jax.ShapeDtypeStruct((nb, seq, n_heads * HEAD_DIM), BF16),
        grid=(nb, n_heads, seq // bq),
        in_specs=_attn_specs(bq, seq, dq, dk, q_off, k_off, v_off) + extra_specs,
        out_specs=pl.BlockSpec((1, bq, HEAD_DIM), lambda b, h, i: (b, i, h)),
        scratch_shapes=scratch,
        compiler_params=_params("parallel", "parallel", "arbitrary"), name=name,
    )(q_arr, k_arr, v_arr, gain.reshape(n_heads, 1, HEAD_DIM).astype(F32), *extra_in)


def _stat_scratch(bq, n):
    out = []
    for _ in range(n):
        out += [pltpu.VMEM((bq, 1), F32), pltpu.VMEM((bq, 1), F32), pltpu.VMEM((bq, HEAD_DIM), F32)]
    return out


def sb_attention(qkv, gain, q_off, k_off, v_off):
    bq = min(ATTN_BLOCK, qkv.shape[1])
    u = jnp.tril(jnp.ones((bq, bq), F32), -1).astype(BF16)
    return _attn_call(_sb_attn_kernel, N_SB, qkv, qkv, qkv, gain, [u],
                      [pl.BlockSpec((bq, bq), lambda b, h, i: (0, 0))],
                      [pltpu.VMEM((bq, 1), F32), pltpu.VMEM((bq, HEAD_DIM), F32)],
                      HEAD_DIM, HEAD_DIM, q_off, k_off, v_off, "sb_attention")


def diff_attention(qkv, gain, t5_table, lam, out_scale, q_off, k_off, v_off):
    bq = min(ATTN_BLOCK, qkv.shape[1])
    assert bq >= MAX_DISTANCE
    dist = jnp.arange(2 * bq, dtype=jnp.int32)
    f_table = t5_table.astype(F32)[_t5_bucket(dist)].T
    rel = jnp.arange(bq)[:, None] - jnp.arange(bq)[None, :]
    b_diag = f_table[:, jnp.maximum(rel, 0)]
    b_sub = f_table[:, rel + bq]
    far = jnp.broadcast_to(t5_table.astype(F32)[N_BUCKETS - 1][:, None, None], (N_DIFF, 1, HEAD_DIM))
    lam_row = jnp.broadcast_to(lam.astype(F32).reshape(1, 1), (1, HEAD_DIM))
    blk = pl.BlockSpec((1, bq, bq), lambda b, h, i: (h, 0, 0))
    return _attn_call(functools.partial(_diff_attn_kernel, out_scale=out_scale), N_DIFF, qkv, qkv, qkv,
                      gain, [b_diag, b_sub, far, lam_row],
                      [blk, blk, pl.BlockSpec((1, 1, HEAD_DIM), lambda b, h, i: (h, 0, 0)),
                       pl.BlockSpec((1, HEAD_DIM), lambda b, h, i: (0, 0))],
                      _stat_scratch(bq, 2), HEAD_DIM, HEAD_DIM, q_off, k_off, v_off, "diff_attention")


def mla_attention(q_cat, k_cat, v, gain):
    bq = min(ATTN_BLOCK, q_cat.shape[1])
    return _attn_call(functools.partial(_softmax_attn_kernel, fox=False), N_MLA, q_cat, k_cat, v, gain,
                      [], [], _stat_scratch(bq, 1), MLA_QK_PAD, MLA_QK_PAD, 0, 0, 0, "mla_attention")


def fox_attention(qkv, gain, c, q_off, k_off, v_off):
    nb, seq, _ = qkv.shape
    bq = min(ATTN_BLOCK, seq)
    c_t = jnp.moveaxis(c, -1, 1)
    c_col = c_t[..., None]
    c_row = c_t.reshape(nb, N_FOX, seq // bq, bq)
    return _attn_call(functools.partial(_softmax_attn_kernel, fox=True), N_FOX, qkv, qkv, qkv, gain,
                      [c_col, c_row],
                      [pl.BlockSpec((1, 1, bq, 1), lambda b, h, i: (b, h, i, 0)),
                       pl.BlockSpec((1, 1, seq // bq, bq), lambda b, h, i: (b, h, 0, 0))],
                      _stat_scratch(bq, 1), HEAD_DIM, HEAD_DIM, q_off, k_off, v_off, "fox_attention")


def _xattn_kernel(q_ref, k_ref, v_ref, o_ref):
    for h in range(N_XHEADS):
        sl = slice(h * HEAD_DIM, (h + 1) * HEAD_DIM)
        s = _qk(q_ref[0, :, sl], k_ref[0, :, sl])
        p = jnp.exp(s - jnp.max(s, axis=-1, keepdims=True))
        o = jnp.dot(p.astype(BF16), v_ref[0, :, sl], preferred_element_type=F32)
        o_ref[0, :, sl] = (o / jnp.sum(p, axis=-1, keepdims=True)).astype(o_ref.dtype)


def cross_attention(q, k, v, bq=512):
    nb, seq, w = q.shape
    n_mem = k.shape[1]
    bq = min(bq, seq)
    return pl.pallas_call(
        _xattn_kernel,
        out_shape=jax.ShapeDtypeStruct((nb, seq, w), BF16),
        grid=(nb, seq // bq),
        in_specs=[pl.BlockSpec((1, bq, w), lambda b, i: (b, i, 0)),
                  pl.BlockSpec((1, n_mem, w), lambda b, i: (b, 0, 0)),
                  pl.BlockSpec((1, n_mem, w), lambda b, i: (b, 0, 0))],
        out_specs=pl.BlockSpec((1, bq, w), lambda b, i: (b, i, 0)),
        compiler_params=_params("parallel", "parallel"), name="cross_attention",
    )(q, k, v)


def _t5_bucket(rel):
    n = jnp.maximum(rel, 0)
    max_exact = N_BUCKETS // 2
    nf = jnp.maximum(n, 1).astype(F32)
    large = max_exact + (jnp.log(nf / max_exact) / math.log(MAX_DISTANCE / max_exact)
                         * (N_BUCKETS - max_exact)).astype(jnp.int32)
    large = jnp.minimum(large, N_BUCKETS - 1)
    return jnp.where(n < max_exact, n, large)


def _rope(x, cos, sin):
    x1, x2 = jnp.split(x, 2, axis=-1)
    return jnp.concatenate([x1 * cos - x2 * sin, x2 * cos + x1 * sin], axis=-1)


def _mixer(h, nb, seq, cos, sin, t5_table, w_in, q_norm, w_uq, kv_norm, w_ukv, lam_vecs, b_forget,
           head_norm, w_out, layer_idx):
    d = h.shape[1]
    hw = N_SB * HEAD_DIM
    o_df = 3 * hw
    o_cq = 6 * hw
    o_ckv = o_cq + MLA_Q_RANK
    o_kr = o_ckv + MLA_KV_RANK
    o_fx = o_kr + MLA_ROPE
    o_fl = o_fx + 3 * hw
    sc = HEAD_DIM ** -0.5
    sc_df = (HEAD_DIM // 2) ** -0.5
    sc_mla = (MLA_NOPE + MLA_ROPE) ** -0.5
    w_main = jnp.concatenate([
        w_in[:, 0:hw] * sc, w_in[:, hw:3 * hw],
        w_in[:, o_df:o_df + hw] * sc_df, w_in[:, o_df + hw:o_df + 3 * hw],
        w_in[:, o_fx:o_fx + hw] * sc, w_in[:, o_fx + hw:o_fx + 3 * hw]], axis=1).astype(BF16)
    n_small = MLA_Q_RANK + MLA_KV_RANK + MLA_ROPE + N_FOX
    pad = (-n_small) % 128
    w_small = jnp.concatenate([w_in[:, o_cq:o_fx], w_in[:, o_fl:o_fl + N_FOX],
                               jnp.zeros((d, pad), F32)], axis=1).astype(BF16)
    qkv = matmul(h, w_main, BF16).reshape(nb, seq, 9 * hw)
    small = matmul(h, w_small, F32, bm=512, bn=n_small + pad)
    c_q = small[:, :MLA_Q_RANK]
    c_kv = small[:, MLA_Q_RANK:MLA_Q_RANK + MLA_KV_RANK]
    k_r = small[:, MLA_Q_RANK + MLA_KV_RANK:MLA_Q_RANK + MLA_KV_RANK + MLA_ROPE]
    f_logit = small[:, n_small - N_FOX:n_small]

    g_sb, g_df, g_mla, g_fx = jnp.split(head_norm, [N_SB, N_SB + N_DIFF, N_SB + N_DIFF + N_MLA], axis=0)

    o_sb = sb_attention(qkv, g_sb, 0, N_SB, 2 * N_SB)

    lv = lam_vecs.astype(F32)
    lam_init = 0.8 - 0.6 * math.exp(-0.3 * layer_idx)
    lam = jnp.exp(jnp.sum(lv[0] * lv[1])) - jnp.exp(jnp.sum(lv[2] * lv[3])) + lam_init
    o_dfa = diff_attention(qkv, g_df, t5_table, lam, 1.0 - lam_init, 3 * N_SB, 4 * N_SB, 5 * N_SB)

    q_c = matmul(rmsnorm(c_q, q_norm), (w_uq * sc_mla).astype(BF16), F32)
    q_c = q_c.reshape(nb, seq, N_MLA, MLA_NOPE + MLA_ROPE)
    q_rope = _rope(q_c[..., MLA_NOPE:], cos[:, :, None, :], sin[:, :, None, :])
    zpad = jnp.zeros((nb, seq, N_MLA, MLA_QK_PAD - MLA_NOPE - MLA_ROPE), F32)
    q_cat = jnp.concatenate([q_c[..., :MLA_NOPE], q_rope, zpad], axis=-1)
    q_cat = q_cat.reshape(nb, seq, N_MLA * MLA_QK_PAD).astype(BF16)
    kv = matmul(rmsnorm(c_kv, kv_norm), w_ukv.astype(BF16), F32).reshape(nb, seq, N_MLA, MLA_NOPE + HEAD_DIM)
    k_rope = _rope(k_r.reshape(nb, seq, MLA_ROPE), cos, sin)
    k_rope = jnp.broadcast_to(k_rope[:, :, None, :], (nb, seq, N_MLA, MLA_ROPE))
    k_cat = jnp.concatenate([kv[..., :MLA_NOPE], k_rope, zpad], axis=-1)
    k_cat = k_cat.reshape(nb, seq, N_MLA * MLA_QK_PAD).astype(BF16)
    v_c = kv[..., MLA_NOPE:].reshape(nb, seq, N_MLA * HEAD_DIM).astype(BF16)
    o_mla = mla_attention(q_cat, k_cat, v_c, g_mla)

    log_f = jax.nn.log_sigmoid(f_logit.reshape(nb, seq, N_FOX) + b_forget.astype(F32))
    c = lax.cumsum(log_f, axis=1)
    o_fx_ = fox_attention(qkv, g_fx, c, 6 * N_SB, 7 * N_SB, 8 * N_SB)

    o = jnp.concatenate([o_sb, o_dfa, o_mla, o_fx_], axis=-1).reshape(nb * seq, -1)
    return matmul(o, w_out.astype(BF16), F32)


def kernel(x, mem, positions, t5_table, mix_norm_pre, w_in, mla_q_norm, w_uq, mla_kv_norm, w_ukv, diff_lambda, b_forget, head_norm, w_out, mix_norm_post, xattn_norm_pre, mem_norm, w_xq, w_xk, w_xv, w_xo, xattn_norm_post, ffn_norm_pre, w_gate, w_up, conv_w, conv_b, w_down, ffn_norm_post):
    nb, seq, d = x.shape
    depth = w_in.shape[0]
    n_mem = mem.shape[1]
    d_ff = w_gate.shape[2]
    xw = N_XHEADS * HEAD_DIM
    inv = ROPE_THETA ** (-jnp.arange(0, MLA_ROPE, 2, dtype=F32) / MLA_ROPE)
    ang = positions.astype(F32)[..., None] * inv
    cos, sin = jnp.cos(ang), jnp.sin(ang)

    xf = x.reshape(nb * seq, d)
    memf = mem.reshape(nb * n_mem, d)
    h = rmsnorm(xf, mix_norm_pre[0])
    for l in range(depth):
        y = _mixer(h, nb, seq, cos, sin, t5_table, w_in[l], mla_q_norm[l], w_uq[l], mla_kv_norm[l],
                   w_ukv[l], diff_lambda[l], b_forget[l], head_norm[l], w_out[l], l)
        xf, h = addnorm(y, xf, mix_norm_post[l], xattn_norm_pre[l])

        mem_n = rmsnorm(memf, mem_norm[l])
        q = matmul(h, (w_xq[l] * HEAD_DIM ** -0.5).astype(BF16), BF16).reshape(nb, seq, xw)
        k = matmul(mem_n, w_xk[l].astype(BF16), BF16).reshape(nb, n_mem, xw)
        v = matmul(mem_n, w_xv[l].astype(BF16), BF16).reshape(nb, n_mem, xw)
        o = cross_attention(q, k, v).reshape(nb * seq, xw)
        y = matmul(o, w_xo[l].astype(BF16), F32)
        xf, h = addnorm(y, xf, xattn_norm_post[l], ffn_norm_pre[l])

        fpad = (-d_ff) % FFN_PAD
        pad_c = lambda w: jnp.pad(w, ((0, 0), (0, fpad)))
        act = ffn_up(h, pad_c(w_gate[l]).astype(BF16), pad_c(w_up[l]).astype(BF16), pad_c(conv_w[l]),
                     jnp.pad(conv_b[l], (0, fpad)), seq)
        w_dn = jnp.pad(w_down[l], ((0, fpad), (0, 0))).astype(BF16)
        y = matmul(act, w_dn, F32, bk=min(FFN_PAD, d_ff + fpad))
        g_next = mix_norm_pre[l + 1] if l + 1 < depth else None
        xf, h = addnorm(y, xf, ffn_norm_post[l], g_next)
    return xf.reshape(nb, seq, d)
```

```python
import functools
import math

import jax
import jax.numpy as jnp
from jax import lax
from jax.experimental import pallas as pl
from jax.experimental.pallas import tpu as pltpu

F32 = jnp.float32
BF16 = jnp.bfloat16

HEAD_DIM = 128
N_SB = 8
N_DIFF = 8
N_MLA = 8
N_FOX = 8
MLA_Q_RANK = 768
MLA_KV_RANK = 512
MLA_NOPE = 128
MLA_ROPE = 64
MLA_QK_PAD = 256
N_XHEADS = 4
N_BUCKETS = 32
MAX_DISTANCE = 128
ROPE_THETA = 10000.0
EPS = 1e-6
NEG_BIG = -1e30
LOG2E = math.log2(math.e)

V7X_VMEM_LIMIT_BYTES = 56 * 1024 * 1024
ATTN_KEY_BLOCK = 512
SB_GROUP = 256
ROW_CHUNK = 128
FFN_PAD = 1024


def _params(*sem, flags=None):
    return pltpu.CompilerParams(dimension_semantics=sem, vmem_limit_bytes=V7X_VMEM_LIMIT_BYTES, flags=flags)


ATTN_FLAGS = None


def _rms(x, g):
    return x * lax.rsqrt(jnp.mean(x * x, axis=-1, keepdims=True) + EPS) * g


def _rmsnorm_kernel(x_ref, g_ref, o_ref):
    o_ref[...] = _rms(x_ref[...].astype(F32), g_ref[...]).astype(o_ref.dtype)


def rmsnorm(x, g, out_dtype=BF16, bm=512):
    m, d = x.shape
    bm = min(bm, m)
    return pl.pallas_call(
        _rmsnorm_kernel,
        out_shape=jax.ShapeDtypeStruct((m, d), out_dtype),
        grid=(pl.cdiv(m, bm),),
        in_specs=[pl.BlockSpec((bm, d), lambda i: (i, 0)), pl.BlockSpec((1, d), lambda i: (0, 0))],
        out_specs=pl.BlockSpec((bm, d), lambda i: (i, 0)),
        compiler_params=_params("parallel"),
        name="rmsnorm",
    )(x, g.reshape(1, d).astype(F32))


def _addnorm_kernel(y_ref, x_ref, gpost_ref, gnext_ref, xo_ref, ho_ref):
    xn = x_ref[...] + _rms(y_ref[...], gpost_ref[...])
    xo_ref[...] = xn
    ho_ref[...] = _rms(xn, gnext_ref[...]).astype(ho_ref.dtype)


def _addnorm_last_kernel(y_ref, x_ref, gpost_ref, xo_ref):
    xo_ref[...] = x_ref[...] + _rms(y_ref[...], gpost_ref[...])


def addnorm(y, x, g_post, g_next=None, bm=256):
    m, d = x.shape
    bm = min(bm, m)
    row = pl.BlockSpec((bm, d), lambda i: (i, 0))
    vec = pl.BlockSpec((1, d), lambda i: (0, 0))
    if g_next is None:
        return pl.pallas_call(
            _addnorm_last_kernel,
            out_shape=jax.ShapeDtypeStruct((m, d), F32),
            grid=(pl.cdiv(m, bm),), in_specs=[row, row, vec], out_specs=row,
            compiler_params=_params("parallel"), name="addnorm_last",
        )(y, x, g_post.reshape(1, d)), None
    return pl.pallas_call(
        _addnorm_kernel,
        out_shape=(jax.ShapeDtypeStruct((m, d), F32), jax.ShapeDtypeStruct((m, d), BF16)),
        grid=(pl.cdiv(m, bm),), in_specs=[row, row, vec, vec], out_specs=(row, row),
        compiler_params=_params("parallel"), name="addnorm",
    )(y, x, g_post.reshape(1, d), g_next.reshape(1, d))


def _matmul_kernel(a_ref, b_ref, o_ref):
    o_ref[...] = jnp.dot(a_ref[...], b_ref[...], preferred_element_type=F32).astype(o_ref.dtype)


def _matmul_acc_kernel(a_ref, b_ref, o_ref, acc_ref, *, nk):
    k = pl.program_id(2)

    @pl.when(k == 0)
    def _():
        acc_ref[...] = jnp.zeros_like(acc_ref)

    acc_ref[...] += jnp.dot(a_ref[...], b_ref[...], preferred_element_type=F32)

    @pl.when(k == nk - 1)
    def _():
        o_ref[...] = acc_ref[...].astype(o_ref.dtype)


def _pick_block(n, target):
    if n <= target:
        return n
    best = 128
    for cand in range(128, target + 1, 128):
        if n % cand == 0:
            best = cand
    assert n % best == 0
    return best


def matmul(a, b, out_dtype, bm=1024, bn=1024, bk=None):
    m, kdim = a.shape
    n = b.shape[1]
    bm, bn = _pick_block(m, bm), _pick_block(n, bn)
    bk = kdim if bk is None else bk
    assert kdim % bk == 0
    nk = kdim // bk
    if nk == 1:
        return pl.pallas_call(
            _matmul_kernel,
            out_shape=jax.ShapeDtypeStruct((m, n), out_dtype),
            grid=(pl.cdiv(m, bm), pl.cdiv(n, bn)),
            in_specs=[pl.BlockSpec((bm, kdim), lambda i, j: (i, 0)),
                      pl.BlockSpec((kdim, bn), lambda i, j: (0, j))],
            out_specs=pl.BlockSpec((bm, bn), lambda i, j: (i, j)),
            compiler_params=_params("parallel", "parallel"), name="matmul",
        )(a, b)
    return pl.pallas_call(
        functools.partial(_matmul_acc_kernel, nk=nk),
        out_shape=jax.ShapeDtypeStruct((m, n), out_dtype),
        grid=(pl.cdiv(m, bm), pl.cdiv(n, bn), nk),
        in_specs=[pl.BlockSpec((bm, bk), lambda i, j, k: (i, k)),
                  pl.BlockSpec((bk, bn), lambda i, j, k: (k, j))],
        out_specs=pl.BlockSpec((bm, bn), lambda i, j, k: (i, j)),
        scratch_shapes=[pltpu.VMEM((bm, bn), F32)],
        compiler_params=_params("parallel", "parallel", "arbitrary"), name="matmul_acc",
    )(a, b)


def _ffn_up_kernel(h_ref, wg_ref, wu_ref, cw_ref, cb_ref, o_ref, carry_ref, *, tiles_per_seq):
    i = pl.program_id(1)
    h = h_ref[...]
    gate = jnp.dot(h, wg_ref[...], preferred_element_type=F32)
    up = jnp.dot(h, wu_ref[...], preferred_element_type=F32)
    bm = gate.shape[0]

    @pl.when(i % tiles_per_seq == 0)
    def _():
        carry_ref[...] = jnp.zeros_like(carry_ref)

    prev = carry_ref[...]
    carry_ref[...] = gate[bm - 8:, :]
    g1 = pltpu.roll(gate, 1, axis=0)
    g2 = pltpu.roll(gate, 2, axis=0)
    row8 = lax.broadcasted_iota(jnp.int32, (8, 1), 0)
    p1 = pltpu.roll(prev, 1, axis=0)
    p2 = pltpu.roll(prev, 2, axis=0)
    g1 = jnp.concatenate([jnp.where(row8 < 1, p1, g1[:8]), g1[8:]], axis=0)
    g2 = jnp.concatenate([jnp.where(row8 < 2, p2, g2[:8]), g2[8:]], axis=0)
    cw = cw_ref[...]
    conv = cw[0:1] * g2 + cw[1:2] * g1 + cw[2:3] * gate + cb_ref[...]
    o_ref[...] = (jax.nn.gelu(conv, approximate=True) * up).astype(o_ref.dtype)


def ffn_up(h, wg, wu, conv_w, conv_b, seq, bm=1024, bn=512):
    m, d = h.shape
    f = wg.shape[1]
    bm = _pick_block(seq, bm)
    bn = _pick_block(f, bn)
    assert seq % bm == 0 and m % seq == 0 and bm % 8 == 0
    return pl.pallas_call(
        functools.partial(_ffn_up_kernel, tiles_per_seq=seq // bm),
        out_shape=jax.ShapeDtypeStruct((m, f), BF16),
        grid=(pl.cdiv(f, bn), m // bm),
        in_specs=[pl.BlockSpec((bm, d), lambda j, i: (i, 0)),
                  pl.BlockSpec((d, bn), lambda j, i: (0, j)),
                  pl.BlockSpec((d, bn), lambda j, i: (0, j)),
                  pl.BlockSpec((3, bn), lambda j, i: (0, j)),
                  pl.BlockSpec((1, bn), lambda j, i: (0, j))],
        out_specs=pl.BlockSpec((bm, bn), lambda j, i: (i, j)),
        scratch_shapes=[pltpu.VMEM((8, bn), F32)],
        compiler_params=_params("arbitrary", "arbitrary"), name="ffn_up",
    )(h, wg, wu, conv_w.astype(F32), conv_b.reshape(1, f).astype(F32))


def _qk(q, k):
    return lax.dot_general(q, k, (((1,), (1,)), ((), ())), preferred_element_type=F32)


def _lanes(x, n):
    return x if n == HEAD_DIM else jnp.concatenate([x] * (n // HEAD_DIM), axis=1)


def _rows(ref, j, bk):
    return ref[pl.ds(pl.multiple_of(j * bk, bk), bk), :]


def _causal_mask(bq, bk, col_off, strict=False):
    row = lax.broadcasted_iota(jnp.int32, (bq, bk), 0)
    col = lax.broadcasted_iota(jnp.int32, (bq, bk), 1) + col_off
    return col < row if strict else col <= row


def _softmax_update(s_ref, v_aug, m_ref, acc_ref, bias=None, mask_off=None, row_start=0):
    bq, bk = s_ref.shape
    live = slice(row_start, bq)
    if bias is None:
        s = s_ref[live, :]
    else:
        s = jnp.concatenate([s_ref[r0:r0 + bk, :] + bias[r0 // bk] for r0 in range(row_start, bq, bk)], axis=0)
    if mask_off is not None:
        row = lax.broadcasted_iota(jnp.int32, s.shape, 0) + row_start
        col = lax.broadcasted_iota(jnp.int32, s.shape, 1) + mask_off
        s = jnp.where(col <= row, s, NEG_BIG)
    m_prev = m_ref[live, :]
    m_new = jnp.maximum(m_prev, jnp.max(s, axis=-1, keepdims=True))
    alpha = jnp.exp2(m_prev - m_new)
    p = jnp.exp2(s - _lanes(m_new, bk))
    pv = jnp.dot(p.astype(BF16), v_aug, preferred_element_type=F32)
    acc_ref[live, :] = _lanes(alpha, 2 * HEAD_DIM) * acc_ref[live, :] + pv
    m_ref[live, :] = m_new


def _init_stats(m_ref, acc_ref):
    m_ref[...] = jnp.full_like(m_ref, NEG_BIG)
    acc_ref[...] = jnp.zeros_like(acc_ref)


def _normalised(acc_ref):
    acc = acc_ref[...]
    return acc[:, :HEAD_DIM] / acc[:, HEAD_DIM:]


def _fill_v_aug(vaug_ref, v_ref):
    vaug_ref[:, :HEAD_DIM] = v_ref[0]
    vaug_ref[:, HEAD_DIM:] = jnp.ones((vaug_ref.shape[0], HEAD_DIM), BF16)


def _softmax_attn_kernel(q_ref, k_ref, v_ref, g_ref, kx_ref, _, o_ref, kaug_ref, vaug_ref, m_ref, acc_ref,
                         sa_ref, sb_ref, *, fox, bq, bk):
    i = pl.program_id(2)

    @pl.when(i == 0)
    def _():
        kaug_ref[:, :HEAD_DIM] = k_ref[0]
        kaug_ref[:, HEAD_DIM:] = kx_ref[0, 0]
        _fill_v_aug(vaug_ref, v_ref)

    q = q_ref[0]
    if fox:
        lane = lax.broadcasted_iota(jnp.int32, (bq, HEAD_DIM), 1)
        q = jnp.concatenate([q, jnp.where(lane < 3, 1.0, 0.0).astype(BF16)], axis=1)
    _init_stats(m_ref, acc_ref)

    def logits(s_ref, j, row_start=0):
        s_ref[row_start:, :] = _qk(q[row_start:], _rows(kaug_ref, j, bk))

    def update(s_ref, j, **kw):
        _softmax_update(s_ref, _rows(vaug_ref, j, bk), m_ref, acc_ref, **kw)

    logits(sa_ref, 0)

    def pair(t, c):
        j = 2 * t
        logits(sb_ref, j + 1)
        update(sa_ref, j)
        logits(sa_ref, j + 2)
        update(sb_ref, j + 1)
        return c

    lax.fori_loop(0, i, pair, 0)
    logits(sb_ref, 2 * i + 1, row_start=bk)
    update(sa_ref, 2 * i, mask_off=0)
    update(sb_ref, 2 * i + 1, mask_off=bk, row_start=bk)
    o_ref[0] = _rms(_normalised(acc_ref), g_ref[0]).astype(o_ref.dtype)


def _diff_attn_kernel(q_ref, k_ref, v_ref, g_ref, bdiag_ref, bsub_ref, far_ref, lam_ref, _, o_ref,
                      vaug_ref, m1_ref, a1_ref, m2_ref, a2_ref, s1a_ref, s2a_ref, s1b_ref, s2b_ref,
                      *, bq, bk, out_scale):
    i = pl.program_id(2)

    @pl.when(i == 0)
    def _():
        _fill_v_aug(vaug_ref, v_ref)

    q = q_ref[0]
    half = lax.broadcasted_iota(jnp.int32, q.shape, 1) < (HEAD_DIM // 2)
    zero = jnp.zeros_like(q)
    q1 = jnp.where(half, q, zero)
    q2 = jnp.where(half, zero, q)
    _init_stats(m1_ref, a1_ref)
    _init_stats(m2_ref, a2_ref)
    k_ref = k_ref.at[0]
    buf_a = (s1a_ref, s2a_ref)
    buf_b = (s1b_ref, s2b_ref)

    def logits(buf, j, row_start=0):
        k = _rows(k_ref, j, bk)
        buf[0][row_start:, :] = _qk(q1[row_start:], k)
        buf[1][row_start:, :] = _qk(q2[row_start:], k)

    far = _lanes(far_ref[0], bk)

    def update(buf, j, bias_top, bias_bottom, **kw):
        v = _rows(vaug_ref, j, bk)
        bias = None if bias_top is None and bias_bottom is None else (bias_top, bias_bottom)
        for s_ref, m_ref, a_ref in ((buf[0], m1_ref, a1_ref), (buf[1], m2_ref, a2_ref)):
            _softmax_update(s_ref, v, m_ref, a_ref, bias=bias, **kw)

    logits(buf_a, 0)

    def pair(t, c):
        j = 2 * t
        logits(buf_b, j + 1)
        update(buf_a, j, None, None)
        logits(buf_a, j + 2)
        update(buf_b, j + 1, None, None)
        return c

    lax.fori_loop(0, i - 1, pair, 0)
    m1_ref[...] = m1_ref[...] + far_ref[0]
    m2_ref[...] = m2_ref[...] + far_ref[0]

    @pl.when(i >= 1)
    def _():
        logits(buf_b, 2 * i - 1)
        update(buf_a, 2 * i - 2, far, far)
        logits(buf_a, 2 * i)
        update(buf_b, 2 * i - 1, bsub_ref[0], far)

    logits(buf_b, 2 * i + 1, row_start=bk)
    update(buf_a, 2 * i, bdiag_ref[0], bsub_ref[0], mask_off=0)
    update(buf_b, 2 * i + 1, None, bdiag_ref[0], mask_off=bk, row_start=bk)
    o = _normalised(a1_ref) - lam_ref[...] * _normalised(a2_ref)
    o_ref[0] = (_rms(o, g_ref[0]) * out_scale).astype(o_ref.dtype)


def _sb_attn_kernel(q_ref, k_ref, v_ref, g_ref, u_ref, _, o_ref, r_ref, acc_ref, za_ref, zb_ref, *, bq, bk):
    i = pl.program_id(2)
    q = q_ref[0]
    k_ref = k_ref.at[0]
    v_ref = v_ref.at[0]
    r_ref[...] = jnp.zeros_like(r_ref)
    acc_ref[...] = jnp.zeros_like(acc_ref)
    u2 = u_ref[...]
    n_groups = bk // SB_GROUP
    sign_bit = jnp.uint32(0x80000000)

    def logits(z_ref, j):
        z_ref[...] = _qk(q, _rows(k_ref, jnp.maximum(j, 0), bk))

    def step(z_ref, j, mask_off=None):
        z = z_ref[...]
        r = r_ref[...]
        parts = [None] * n_groups
        for g in reversed(range(n_groups)):
            zg = z[:, g * SB_GROUP:(g + 1) * SB_GROUP]
            neg_abs = lax.bitcast_convert_type(lax.bitcast_convert_type(zg, jnp.uint32) | sign_bit, F32)
            sp = jnp.maximum(zg, 0.0) + jnp.log2(1.0 + jnp.exp2(neg_abs))
            if mask_off is not None:
                strict = _causal_mask(bq, SB_GROUP, mask_off + g * SB_GROUP, strict=True)
                sp = jnp.where(strict, sp, 0.0)
            hi = sp.astype(BF16)
            lo = (sp - hi.astype(F32)).astype(BF16)
            between = jnp.dot(jnp.concatenate([hi, lo], axis=1), u2, preferred_element_type=F32)
            a = jnp.exp2((zg - sp) + between + _lanes(r, SB_GROUP))
            if mask_off is not None:
                a = jnp.where(strict, a, 0.0)
            parts[g] = a.astype(BF16)
            r = r - jnp.sum(sp, axis=-1, keepdims=True)
        r_ref[...] = r
        acc_ref[...] += jnp.dot(jnp.concatenate(parts, axis=1), _rows(v_ref, j, bk),
                                preferred_element_type=F32)

    logits(za_ref, 2 * i + 1)
    logits(zb_ref, 2 * i)
    step(za_ref, 2 * i + 1, mask_off=bk)
    logits(za_ref, 2 * i - 1)
    step(zb_ref, 2 * i, mask_off=0)

    def pair(t, c):
        j = 2 * i - 1 - 2 * t
        logits(zb_ref, j - 1)
        step(za_ref, j)
        logits(za_ref, j - 2)
        step(zb_ref, j - 1)
        return c

    lax.fori_loop(0, i, pair, 0)

    o_ref[0] = _rms(acc_ref[...], g_ref[0]).astype(o_ref.dtype)


def _attn_call(kernel, o_buf, head_off, n_heads, q_arr, k_arr, v_arr, gain, extra_in, extra_specs, scratch,
               dq, q_col, k_col, v_col, name):
    nb, seq, _ = q_arr.shape
    bq, bk = _attn_blocks(seq)
    in_specs = [pl.BlockSpec((1, bq, dq), lambda b, h, i: (b, i, q_col(h))),
                pl.BlockSpec((1, seq, HEAD_DIM), lambda b, h, i: (b, 0, k_col(h))),
                pl.BlockSpec((1, seq, HEAD_DIM), lambda b, h, i: (b, 0, v_col(h))),
                pl.BlockSpec((1, 1, HEAD_DIM), lambda b, h, i: (h, 0, 0))]
    in_specs += extra_specs + [pl.BlockSpec(memory_space=pl.ANY)]
    n_in = len(in_specs)
    return pl.pallas_call(
        functools.partial(kernel, bq=bq, bk=bk),
        out_shape=jax.ShapeDtypeStruct(o_buf.shape, o_buf.dtype),
        grid=(nb, n_heads, seq // bq),
        in_specs=in_specs,
        out_specs=pl.BlockSpec((1, bq, HEAD_DIM), lambda b, h, i: (b, i, head_off + h)),
        scratch_shapes=scratch,
        input_output_aliases={n_in - 1: 0},
        compiler_params=_params("arbitrary", "arbitrary", "arbitrary", flags=ATTN_FLAGS), name=name,
    )(q_arr, k_arr, v_arr, gain.reshape(n_heads, 1, HEAD_DIM).astype(F32), *extra_in, o_buf)


def _attn_blocks(seq):
    bk = min(ATTN_KEY_BLOCK, seq // 2)
    assert seq % (2 * bk) == 0 and bk % SB_GROUP == 0
    return 2 * bk, bk


def _stat_scratch(bq, n):
    out = []
    for _ in range(n):
        out += [pltpu.VMEM((bq, HEAD_DIM), F32), pltpu.VMEM((bq, 2 * HEAD_DIM), F32)]
    return out


def _logit_scratch(bq, bk, n):
    return [pltpu.VMEM((bq, bk), F32)] * n


def _group(off):
    return lambda h: off + h


def sb_attention(o_buf, head_off, qkv, gain, q_off, k_off, v_off):
    bq, bk = _attn_blocks(qkv.shape[1])
    u = -jnp.tril(jnp.ones((SB_GROUP, SB_GROUP), F32), -1).astype(BF16)
    u2 = jnp.concatenate([u, u], axis=0)
    return _attn_call(_sb_attn_kernel, o_buf, head_off, N_SB, qkv, qkv, qkv, gain, [u2],
                      [pl.BlockSpec((2 * SB_GROUP, SB_GROUP), lambda b, h, i: (0, 0))],
                      [pltpu.VMEM((bq, HEAD_DIM), F32), pltpu.VMEM((bq, HEAD_DIM), F32)]
                      + _logit_scratch(bq, bk, 2),
                      HEAD_DIM, _group(q_off), _group(k_off), _group(v_off), "sb_attention")


def _toeplitz(g, n):
    nh = g.shape[0]
    t = jnp.broadcast_to(g[:, None, :], (nh, n, 2 * n - 1))
    t = jnp.pad(t, ((0, 0), (0, 0), (0, 1))).reshape(nh, 2 * n * n)[:, :n * (2 * n - 1)]
    return t.reshape(nh, n, 2 * n - 1)[:, :, n - 1:]


def diff_attention(o_buf, head_off, qkv, gain, t5_table, lam, out_scale, q_off, k_off, v_off):
    seq = qkv.shape[1]
    bq, bk = _attn_blocks(seq)
    assert bk >= MAX_DISTANCE
    table = t5_table.astype(F32) * LOG2E
    f_table = table[_t5_bucket(jnp.arange(2 * bk, dtype=jnp.int32))].T
    x = jnp.arange(2 * bk - 1)
    b_diag = _toeplitz(f_table[:, jnp.maximum(bk - 1 - x, 0)], bk)
    b_sub = _toeplitz(f_table[:, 2 * bk - 1 - x], bk)
    far = jnp.broadcast_to(table[N_BUCKETS - 1][:, None, None], (N_DIFF, 1, HEAD_DIM))
    lam_row = jnp.broadcast_to(lam.astype(F32).reshape(1, 1), (1, HEAD_DIM))
    blk = pl.BlockSpec((1, bk, bk), lambda b, h, i: (h, 0, 0))
    return _attn_call(functools.partial(_diff_attn_kernel, out_scale=out_scale), o_buf, head_off, N_DIFF,
                      qkv, qkv, qkv, gain, [b_diag, b_sub, far, lam_row],
                      [blk, blk, pl.BlockSpec((1, 1, HEAD_DIM), lambda b, h, i: (h, 0, 0)),
                       pl.BlockSpec((1, HEAD_DIM), lambda b, h, i: (0, 0))],
                      [pltpu.VMEM((seq, 2 * HEAD_DIM), BF16)] + _stat_scratch(bq, 2)
                      + _logit_scratch(bq, bk, 4),
                      HEAD_DIM, _group(q_off), _group(k_off), _group(v_off), "diff_attention")


def _wide_softmax_attention(o_buf, head_off, fox, n_heads, q_arr, dq, q_col, kv_arr, k_col, v_col, kx, kx_spec,
                            gain, name):
    seq = q_arr.shape[1]
    bq, bk = _attn_blocks(seq)
    scratch = [pltpu.VMEM((seq, 2 * HEAD_DIM), BF16), pltpu.VMEM((seq, 2 * HEAD_DIM), BF16)]
    return _attn_call(functools.partial(_softmax_attn_kernel, fox=fox), o_buf, head_off, n_heads,
                      q_arr, kv_arr, kv_arr, gain, [kx], [kx_spec],
                      scratch + _stat_scratch(bq, 1) + _logit_scratch(bq, bk, 2),
                      dq, q_col, k_col, v_col, name)


def mla_attention(o_buf, head_off, q_cat, kv, k_rope_pad, gain):
    nb, seq, _ = q_cat.shape
    spec = pl.BlockSpec((1, 1, seq, HEAD_DIM), lambda b, h, i: (b, 0, 0, 0))
    return _wide_softmax_attention(o_buf, head_off, False, N_MLA, q_cat, MLA_QK_PAD, lambda h: h,
                                   kv, lambda h: 2 * h, lambda h: 2 * h + 1,
                                   k_rope_pad.reshape(nb, 1, seq, HEAD_DIM), spec, gain, "mla_attention")


def fox_attention(o_buf, head_off, qkv, gain, c, q_off, k_off, v_off):
    nb, seq, _ = qkv.shape
    c2 = jnp.moveaxis(c, -1, 1) * LOG2E
    hi = c2.astype(BF16)
    mid = (c2 - hi.astype(F32)).astype(BF16)
    lo = (c2 - hi.astype(F32) - mid.astype(F32)).astype(BF16)
    kx = jnp.stack([-hi, -mid, -lo], axis=-1)
    kx = jnp.pad(kx, ((0, 0), (0, 0), (0, 0), (0, HEAD_DIM - 3)))
    spec = pl.BlockSpec((1, 1, seq, HEAD_DIM), lambda b, h, i: (b, h, 0, 0))
    return _wide_softmax_attention(o_buf, head_off, True, N_FOX, qkv, HEAD_DIM, _group(q_off),
                                   qkv, _group(k_off), _group(v_off), kx, spec, gain, "fox_attention")


def _xattn_kernel(q_ref, k_ref, v_ref, o_ref):
    for h in range(N_XHEADS):
        sl = slice(h * HEAD_DIM, (h + 1) * HEAD_DIM)
        s = _qk(q_ref[0, :, sl], k_ref[0, :, sl])
        p = jnp.exp(s - jnp.max(s, axis=-1, keepdims=True))
        o = jnp.dot(p.astype(BF16), v_ref[0, :, sl], preferred_element_type=F32)
        o_ref[0, :, sl] = (o / jnp.sum(p, axis=-1, keepdims=True)).astype(o_ref.dtype)


def cross_attention(q, k, v, bq=512):
    nb, seq, w = q.shape
    n_mem = k.shape[1]
    bq = min(bq, seq)
    return pl.pallas_call(
        _xattn_kernel,
        out_shape=jax.ShapeDtypeStruct((nb, seq, w), BF16),
        grid=(nb, seq // bq),
        in_specs=[pl.BlockSpec((1, bq, w), lambda b, i: (b, i, 0)),
                  pl.BlockSpec((1, n_mem, w), lambda b, i: (b, 0, 0)),
                  pl.BlockSpec((1, n_mem, w), lambda b, i: (b, 0, 0))],
        out_specs=pl.BlockSpec((1, bq, w), lambda b, i: (b, i, 0)),
        compiler_params=_params("parallel", "parallel"), name="cross_attention",
    )(q, k, v)


def _t5_bucket(rel):
    n = jnp.maximum(rel, 0)
    max_exact = N_BUCKETS // 2
    nf = jnp.maximum(n, 1).astype(F32)
    large = max_exact + (jnp.log(nf / max_exact) / math.log(MAX_DISTANCE / max_exact)
                         * (N_BUCKETS - max_exact)).astype(jnp.int32)
    large = jnp.minimum(large, N_BUCKETS - 1)
    return jnp.where(n < max_exact, n, large)


def _rope(x, cos, sin):
    x1, x2 = jnp.split(x, 2, axis=-1)
    return jnp.concatenate([x1 * cos - x2 * sin, x2 * cos + x1 * sin], axis=-1)


def _mixer(h, nb, seq, cos, sin, t5_table, w_in, q_norm, w_uq, kv_norm, w_ukv, lam_vecs, b_forget,
           head_norm, w_out, layer_idx):
    d = h.shape[1]
    hw = N_SB * HEAD_DIM
    o_df = 3 * hw
    o_cq = 6 * hw
    o_ckv = o_cq + MLA_Q_RANK
    o_kr = o_ckv + MLA_KV_RANK
    o_fx = o_kr + MLA_ROPE
    o_fl = o_fx + 3 * hw
    sc = HEAD_DIM ** -0.5 * LOG2E
    sc_df = (HEAD_DIM // 2) ** -0.5 * LOG2E
    sc_mla = (MLA_NOPE + MLA_ROPE) ** -0.5 * LOG2E
    w_main = jnp.concatenate([
        w_in[:, 0:hw] * sc, w_in[:, hw:3 * hw],
        w_in[:, o_df:o_df + hw] * sc_df, w_in[:, o_df + hw:o_df + 3 * hw],
        w_in[:, o_fx:o_fx + hw] * sc, w_in[:, o_fx + hw:o_fx + 3 * hw]], axis=1).astype(BF16)
    n_small = MLA_Q_RANK + MLA_KV_RANK + MLA_ROPE + N_FOX
    pad = (-n_small) % 128
    w_small = jnp.concatenate([w_in[:, o_cq:o_fx], w_in[:, o_fl:o_fl + N_FOX],
                               jnp.zeros((d, pad), F32)], axis=1).astype(BF16)
    qkv = matmul(h, w_main, BF16).reshape(nb, seq, 9 * hw)
    small = matmul(h, w_small, F32, bm=512, bn=n_small + pad)
    c_q = small[:, :MLA_Q_RANK]
    c_kv = small[:, MLA_Q_RANK:MLA_Q_RANK + MLA_KV_RANK]
    k_r = small[:, MLA_Q_RANK + MLA_KV_RANK:MLA_Q_RANK + MLA_KV_RANK + MLA_ROPE]
    f_logit = small[:, n_small - N_FOX:n_small]

    g_sb, g_df, g_mla, g_fx = jnp.split(head_norm, [N_SB, N_SB + N_DIFF, N_SB + N_DIFF + N_MLA], axis=0)
    o = jnp.zeros((nb, seq, (N_SB + N_DIFF + N_MLA + N_FOX) * HEAD_DIM), BF16)

    o = sb_attention(o, 0, qkv, g_sb, 0, N_SB, 2 * N_SB)

    lv = lam_vecs.astype(F32)
    lam_init = 0.8 - 0.6 * math.exp(-0.3 * layer_idx)
    lam = jnp.exp(jnp.sum(lv[0] * lv[1])) - jnp.exp(jnp.sum(lv[2] * lv[3])) + lam_init
    o = diff_attention(o, N_SB, qkv, g_df, t5_table, lam, 1.0 - lam_init, 3 * N_SB, 4 * N_SB, 5 * N_SB)

    q_c = matmul(rmsnorm(c_q, q_norm), (w_uq * sc_mla).astype(BF16), F32)
    q_c = q_c.reshape(nb, seq, N_MLA, MLA_NOPE + MLA_ROPE)
    q_rope = _rope(q_c[..., MLA_NOPE:], cos[:, :, None, :], sin[:, :, None, :])
    zpad = jnp.zeros((nb, seq, N_MLA, MLA_QK_PAD - MLA_NOPE - MLA_ROPE), F32)
    q_cat = jnp.concatenate([q_c[..., :MLA_NOPE], q_rope, zpad], axis=-1)
    q_cat = q_cat.reshape(nb, seq, N_MLA * MLA_QK_PAD).astype(BF16)
    kv = matmul(rmsnorm(c_kv, kv_norm), w_ukv.astype(BF16), BF16).reshape(nb, seq, N_MLA * 2 * HEAD_DIM)
    k_rope = _rope(k_r.reshape(nb, seq, MLA_ROPE), cos, sin)
    k_rope_pad = jnp.pad(k_rope, ((0, 0), (0, 0), (0, HEAD_DIM - MLA_ROPE))).astype(BF16)
    o = mla_attention(o, N_SB + N_DIFF, q_cat, kv, k_rope_pad, g_mla)

    log_f = jax.nn.log_sigmoid(f_logit.reshape(nb, seq, N_FOX) + b_forget.astype(F32))
    c = lax.cumsum(log_f, axis=1)
    o = fox_attention(o, N_SB + N_DIFF + N_MLA, qkv, g_fx, c, 6 * N_SB, 7 * N_SB, 8 * N_SB)

    return matmul(o.reshape(nb * seq, -1), w_out.astype(BF16), F32)


def kernel(x, mem, positions, t5_table, mix_norm_pre, w_in, mla_q_norm, w_uq, mla_kv_norm, w_ukv, diff_lambda, b_forget, head_norm, w_out, mix_norm_post, xattn_norm_pre, mem_norm, w_xq, w_xk, w_xv, w_xo, xattn_norm_post, ffn_norm_pre, w_gate, w_up, conv_w, conv_b, w_down, ffn_norm_post):
    nb, seq, d = x.shape
    depth = w_in.shape[0]
    n_mem = mem.shape[1]
    d_ff = w_gate.shape[2]
    xw = N_XHEADS * HEAD_DIM
    inv = ROPE_THETA ** (-jnp.arange(0, MLA_ROPE, 2, dtype=F32) / MLA_ROPE)
    ang = positions.astype(F32)[..., None] * inv
    cos, sin = jnp.cos(ang), jnp.sin(ang)

    xf = x.reshape(nb * seq, d)
    memf = mem.reshape(nb * n_mem, d)
    h = rmsnorm(xf, mix_norm_pre[0])
    for l in range(depth):
        y = _mixer(h, nb, seq, cos, sin, t5_table, w_in[l], mla_q_norm[l], w_uq[l], mla_kv_norm[l],
                   w_ukv[l], diff_lambda[l], b_forget[l], head_norm[l], w_out[l], l)
        xf, h = addnorm(y, xf, mix_norm_post[l], xattn_norm_pre[l])

        mem_n = rmsnorm(memf, mem_norm[l])
        q = matmul(h, (w_xq[l] * HEAD_DIM ** -0.5).astype(BF16), BF16).reshape(nb, seq, xw)
        k = matmul(mem_n, w_xk[l].astype(BF16), BF16).reshape(nb, n_mem, xw)
        v = matmul(mem_n, w_xv[l].astype(BF16), BF16).reshape(nb, n_mem, xw)
        o = cross_attention(q, k, v).reshape(nb * seq, xw)
        y = matmul(o, w_xo[l].astype(BF16), F32)
        xf, h = addnorm(y, xf, xattn_norm_post[l], ffn_norm_pre[l])

        fpad = (-d_ff) % FFN_PAD
        pad_c = lambda w: jnp.pad(w, ((0, 0), (0, fpad)))
        act = ffn_up(h, pad_c(w_gate[l]).astype(BF16), pad_c(w_up[l]).astype(BF16), pad_c(conv_w[l]),
                     jnp.pad(conv_b[l], (0, fpad)), seq)
        w_dn = jnp.pad(w_down[l], ((0, fpad), (0, 0))).astype(BF16)
        y = matmul(act, w_dn, F32, bk=min(FFN_PAD, d_ff + fpad))
        g_next = mix_norm_pre[l + 1] if l + 1 < depth else None
        xf, h = addnorm(y, xf, ffn_norm_post[l], g_next)
    return xf.reshape(nb, seq, d)
```

```python
import functools
import math

import jax
import jax.numpy as jnp
from jax import lax
from jax.experimental import pallas as pl
from jax.experimental.pallas import tpu as pltpu

F32 = jnp.float32
BF16 = jnp.bfloat16

HEAD_DIM = 128
N_SB = 8
N_DIFF = 8
N_MLA = 8
N_FOX = 8
MLA_Q_RANK = 768
MLA_KV_RANK = 512
MLA_NOPE = 128
MLA_ROPE = 64
MLA_QK_PAD = 256
N_XHEADS = 4
N_BUCKETS = 32
MAX_DISTANCE = 128
ROPE_THETA = 10000.0
EPS = 1e-6
NEG_BIG = -1e30
LOG2E = math.log2(math.e)

V7X_VMEM_LIMIT_BYTES = 56 * 1024 * 1024
ATTN_KEY_BLOCK = 512
SB_GROUP = 256
ROW_CHUNK = 128
FFN_CHUNK = 256


def _params(*sem, flags=None):
    return pltpu.CompilerParams(dimension_semantics=sem, vmem_limit_bytes=V7X_VMEM_LIMIT_BYTES, flags=flags)


ATTN_FLAGS = None


def _rms(x, g):
    return x * lax.rsqrt(jnp.mean(x * x, axis=-1, keepdims=True) + EPS) * g


def _rmsnorm_kernel(x_ref, g_ref, o_ref):
    o_ref[...] = _rms(x_ref[...].astype(F32), g_ref[...]).astype(o_ref.dtype)


def rmsnorm(x, g, out_dtype=BF16, bm=512):
    m, d = x.shape
    bm = min(bm, m)
    return pl.pallas_call(
        _rmsnorm_kernel,
        out_shape=jax.ShapeDtypeStruct((m, d), out_dtype),
        grid=(pl.cdiv(m, bm),),
        in_specs=[pl.BlockSpec((bm, d), lambda i: (i, 0)), pl.BlockSpec((1, d), lambda i: (0, 0))],
        out_specs=pl.BlockSpec((bm, d), lambda i: (i, 0)),
        compiler_params=_params("parallel"),
        name="rmsnorm",
    )(x, g.reshape(1, d).astype(F32))


def _addnorm_kernel(y_ref, x_ref, gpost_ref, gnext_ref, xo_ref, ho_ref):
    xn = x_ref[...] + _rms(y_ref[...], gpost_ref[...])
    xo_ref[...] = xn
    ho_ref[...] = _rms(xn, gnext_ref[...]).astype(ho_ref.dtype)


def _addnorm_last_kernel(y_ref, x_ref, gpost_ref, xo_ref):
    xo_ref[...] = x_ref[...] + _rms(y_ref[...], gpost_ref[...])


def addnorm(y, x, g_post, g_next=None, bm=256):
    m, d = x.shape
    bm = min(bm, m)
    row = pl.BlockSpec((bm, d), lambda i: (i, 0))
    vec = pl.BlockSpec((1, d), lambda i: (0, 0))
    if g_next is None:
        return pl.pallas_call(
            _addnorm_last_kernel,
            out_shape=jax.ShapeDtypeStruct((m, d), F32),
            grid=(pl.cdiv(m, bm),), in_specs=[row, row, vec], out_specs=row,
            compiler_params=_params("parallel"), name="addnorm_last",
        )(y, x, g_post.reshape(1, d)), None
    return pl.pallas_call(
        _addnorm_kernel,
        out_shape=(jax.ShapeDtypeStruct((m, d), F32), jax.ShapeDtypeStruct((m, d), BF16)),
        grid=(pl.cdiv(m, bm),), in_specs=[row, row, vec, vec], out_specs=(row, row),
        compiler_params=_params("parallel"), name="addnorm",
    )(y, x, g_post.reshape(1, d), g_next.reshape(1, d))


def _matmul_kernel(a_ref, b_ref, o_ref):
    o_ref[...] = jnp.dot(a_ref[...], b_ref[...], preferred_element_type=F32).astype(o_ref.dtype)


def _matmul_acc_kernel(a_ref, b_ref, o_ref, acc_ref, *, nk):
    k = pl.program_id(2)

    @pl.when(k == 0)
    def _():
        acc_ref[...] = jnp.zeros_like(acc_ref)

    acc_ref[...] += jnp.dot(a_ref[...], b_ref[...], preferred_element_type=F32)

    @pl.when(k == nk - 1)
    def _():
        o_ref[...] = acc_ref[...].astype(o_ref.dtype)


def _pick_block(n, target):
    if n <= target:
        return n
    best = 128
    for cand in range(128, target + 1, 128):
        if n % cand == 0:
            best = cand
    assert n % best == 0
    return best


def matmul(a, b, out_dtype, bm=1024, bn=1024, bk=None):
    m, kdim = a.shape
    n = b.shape[1]
    bm, bn = _pick_block(m, bm), _pick_block(n, bn)
    bk = kdim if bk is None else bk
    assert kdim % bk == 0
    nk = kdim // bk
    if nk == 1:
        return pl.pallas_call(
            _matmul_kernel,
            out_shape=jax.ShapeDtypeStruct((m, n), out_dtype),
            grid=(pl.cdiv(m, bm), pl.cdiv(n, bn)),
            in_specs=[pl.BlockSpec((bm, kdim), lambda i, j: (i, 0)),
                      pl.BlockSpec((kdim, bn), lambda i, j: (0, j))],
            out_specs=pl.BlockSpec((bm, bn), lambda i, j: (i, j)),
            compiler_params=_params("parallel", "parallel"), name="matmul",
        )(a, b)
    return pl.pallas_call(
        functools.partial(_matmul_acc_kernel, nk=nk),
        out_shape=jax.ShapeDtypeStruct((m, n), out_dtype),
        grid=(pl.cdiv(m, bm), pl.cdiv(n, bn), nk),
        in_specs=[pl.BlockSpec((bm, bk), lambda i, j, k: (i, k)),
                  pl.BlockSpec((bk, bn), lambda i, j, k: (k, j))],
        out_specs=pl.BlockSpec((bm, bn), lambda i, j, k: (i, j)),
        scratch_shapes=[pltpu.VMEM((bm, bn), F32)],
        compiler_params=_params("parallel", "parallel", "arbitrary"), name="matmul_acc",
    )(a, b)


def _matmul_addnorm_kernel(a_ref, w_ref, x_ref, gpost_ref, gnext_ref, xo_ref, ho_ref):
    y = jnp.dot(a_ref[...], w_ref[...], preferred_element_type=F32)
    xn = x_ref[...] + _rms(y, gpost_ref[...])
    xo_ref[...] = xn
    ho_ref[...] = _rms(xn, gnext_ref[...]).astype(ho_ref.dtype)


def matmul_addnorm(a, w, x, g_post, g_next, bm=256):
    m, kdim = a.shape
    d = w.shape[1]
    bm = _pick_block(m, bm)
    row = pl.BlockSpec((bm, d), lambda i: (i, 0))
    vec = pl.BlockSpec((1, d), lambda i: (0, 0))
    return pl.pallas_call(
        _matmul_addnorm_kernel,
        out_shape=(jax.ShapeDtypeStruct((m, d), F32), jax.ShapeDtypeStruct((m, d), BF16)),
        grid=(m // bm,),
        in_specs=[pl.BlockSpec((bm, kdim), lambda i: (i, 0)), pl.BlockSpec((kdim, d), lambda i: (0, 0)),
                  row, vec, vec],
        out_specs=(row, row),
        compiler_params=_params("parallel"), name="matmul_addnorm",
    )(a, w, x, g_post.reshape(1, d), g_next.reshape(1, d))


def _norm_matmul_kernel(x_ref, g_ref, w_ref, *rest, rope_heads):
    o_ref = rest[-1]
    xn = _rms(x_ref[...], g_ref[...]).astype(BF16)
    y = jnp.dot(xn, w_ref[...], preferred_element_type=F32)
    if not rope_heads:
        o_ref[...] = y.astype(o_ref.dtype)
        return
    cs, sn = rest[0][...], rest[1][...]
    for hd in range(rope_heads):
        c0 = hd * MLA_QK_PAD
        x = y[:, c0 + MLA_NOPE:c0 + MLA_QK_PAD]
        o_ref[:, c0:c0 + MLA_NOPE] = y[:, c0:c0 + MLA_NOPE].astype(o_ref.dtype)
        o_ref[:, c0 + MLA_NOPE:c0 + MLA_QK_PAD] = (x * cs + pltpu.roll(x, MLA_ROPE, axis=1) * sn).astype(o_ref.dtype)


def norm_matmul(x_arr, col_block, kdim, g, w, rope=None, bm=512):
    m = x_arr.shape[0]
    n = w.shape[1]
    bm = _pick_block(m, bm)
    in_specs = [pl.BlockSpec((bm, kdim), lambda i: (i, col_block)),
                pl.BlockSpec((1, kdim), lambda i: (0, 0)),
                pl.BlockSpec((kdim, n), lambda i: (0, 0))]
    args = [x_arr, g.reshape(1, kdim).astype(F32), w]
    if rope is not None:
        in_specs += [pl.BlockSpec((bm, HEAD_DIM), lambda i: (i, 0))] * 2
        args += [rope[0], rope[1]]
    return pl.pallas_call(
        functools.partial(_norm_matmul_kernel, rope_heads=0 if rope is None else rope[2]),
        out_shape=jax.ShapeDtypeStruct((m, n), BF16),
        grid=(m // bm,), in_specs=in_specs,
        out_specs=pl.BlockSpec((bm, n), lambda i: (i, 0)),
        compiler_params=_params("parallel"), name="norm_matmul",
    )(*args)


def _ffn_up_kernel(h_ref, wg_ref, wu_ref, cw_ref, cb_ref, *rest, tiles_per_seq):
    o_ref, carry_ref = rest[-2:]
    i = pl.program_id(1)
    bm = h_ref.shape[0]
    bn = o_ref.shape[1]

    @pl.when(i % tiles_per_seq == 0)
    def _():
        carry_ref[...] = jnp.zeros_like(carry_ref)

    row8 = lax.broadcasted_iota(jnp.int32, (8, 1), 0)
    for c0 in range(0, bn, FFN_CHUNK):
        cols = slice(c0, c0 + FFN_CHUNK)
        gate = jnp.dot(h_ref[...], wg_ref[:, cols], preferred_element_type=F32)
        up = jnp.dot(h_ref[...], wu_ref[:, cols], preferred_element_type=F32)
        prev = carry_ref[:, cols]
        carry_ref[:, cols] = gate[bm - 8:, :]
        g1 = pltpu.roll(gate, 1, axis=0)
        g2 = pltpu.roll(gate, 2, axis=0)
        p1 = pltpu.roll(prev, 1, axis=0)
        p2 = pltpu.roll(prev, 2, axis=0)
        g1 = jnp.concatenate([jnp.where(row8 < 1, p1, g1[:8]), g1[8:]], axis=0)
        g2 = jnp.concatenate([jnp.where(row8 < 2, p2, g2[:8]), g2[8:]], axis=0)
        cw = cw_ref[:, cols]
        conv = cw[0:1] * g2 + cw[1:2] * g1 + cw[2:3] * gate + cb_ref[:, cols]
        o_ref[:, cols] = (jax.nn.gelu(conv, approximate=True) * up).astype(o_ref.dtype)


def _ffn_up_call(h, wg, wu, conv_w, conv_b, seq, out, col0, n_cols, bm, bn):
    m, d = h.shape
    f = wg.shape[1]
    assert n_cols % bn == 0 and col0 % bn == 0 and bn % FFN_CHUNK == 0
    jb = col0 // bn
    in_specs = [pl.BlockSpec((bm, d), lambda j, i: (i, 0)),
                pl.BlockSpec((d, bn), lambda j, i: (0, jb + j)),
                pl.BlockSpec((d, bn), lambda j, i: (0, jb + j)),
                pl.BlockSpec((3, bn), lambda j, i: (0, jb + j)),
                pl.BlockSpec((1, bn), lambda j, i: (0, jb + j))]
    args = [h, wg, wu, conv_w, conv_b]
    aliases = {}
    if out is not None:
        in_specs.append(pl.BlockSpec(memory_space=pl.ANY))
        args.append(out)
        aliases = {5: 0}
    return pl.pallas_call(
        functools.partial(_ffn_up_kernel, tiles_per_seq=seq // bm),
        out_shape=jax.ShapeDtypeStruct((m, f), BF16),
        grid=(n_cols // bn, m // bm),
        in_specs=in_specs,
        out_specs=pl.BlockSpec((bm, bn), lambda j, i: (i, jb + j)),
        scratch_shapes=[pltpu.VMEM((8, bn), F32)],
        input_output_aliases=aliases,
        compiler_params=_params("arbitrary", "arbitrary"), name="ffn_up",
    )(*args)


def ffn_up(h, wg, wu, conv_w, conv_b, seq, bm=1024, bn=512):
    m, d = h.shape
    f = wg.shape[1]
    bm = _pick_block(seq, bm)
    assert seq % bm == 0 and m % seq == 0 and bm % 8 == 0 and f % FFN_CHUNK == 0
    conv_w = conv_w.astype(F32)
    conv_b = conv_b.reshape(1, f).astype(F32)
    bn = min(bn, f)
    main = (f // bn) * bn
    out = _ffn_up_call(h, wg, wu, conv_w, conv_b, seq, None, 0, main, bm, bn)
    if main < f:
        out = _ffn_up_call(h, wg, wu, conv_w, conv_b, seq, out, main, f - main, bm, _pick_block(f - main, bn))
    return out


def _qk(q, k):
    return lax.dot_general(q, k, (((1,), (1,)), ((), ())), preferred_element_type=F32)


def _lanes(x, n):
    return x if n == HEAD_DIM else jnp.concatenate([x] * (n // HEAD_DIM), axis=1)


def _rows(ref, j, bk):
    return ref[pl.ds(pl.multiple_of(j * bk, bk), bk), :]


def _causal_mask(bq, bk, col_off, strict=False):
    row = lax.broadcasted_iota(jnp.int32, (bq, bk), 0)
    col = lax.broadcasted_iota(jnp.int32, (bq, bk), 1) + col_off
    return col < row if strict else col <= row


def _softmax_update(s_ref, v_aug, m_ref, acc_ref, bias=None, mask_off=None, row_start=0):
    bq, bk = s_ref.shape
    live = slice(row_start, bq)
    if bias is None:
        s = s_ref[live, :]
    else:
        s = jnp.concatenate([s_ref[r0:r0 + bk, :] + bias[r0 // bk] for r0 in range(row_start, bq, bk)], axis=0)
    if mask_off is not None:
        row = lax.broadcasted_iota(jnp.int32, s.shape, 0) + row_start
        col = lax.broadcasted_iota(jnp.int32, s.shape, 1) + mask_off
        s = jnp.where(col <= row, s, NEG_BIG)
    m_prev = m_ref[live, :]
    m_new = jnp.maximum(m_prev, jnp.max(s, axis=-1, keepdims=True))
    alpha = jnp.exp2(m_prev - m_new)
    p = jnp.exp2(s - _lanes(m_new, bk))
    pv = jnp.dot(p.astype(BF16), v_aug, preferred_element_type=F32)
    acc_ref[live, :] = _lanes(alpha, 2 * HEAD_DIM) * acc_ref[live, :] + pv
    m_ref[live, :] = m_new


def _init_stats(m_ref, acc_ref):
    m_ref[...] = jnp.full_like(m_ref, NEG_BIG)
    acc_ref[...] = jnp.zeros_like(acc_ref)


def _normalised(acc_ref):
    acc = acc_ref[...]
    return acc[:, :HEAD_DIM] / acc[:, HEAD_DIM:]


def _fill_v_aug(vaug_ref, v_ref):
    vaug_ref[:, :HEAD_DIM] = v_ref[0]
    vaug_ref[:, HEAD_DIM:] = jnp.ones((vaug_ref.shape[0], HEAD_DIM), BF16)


def _softmax_attn_kernel(q_ref, k_ref, v_ref, g_ref, kx_ref, _, o_ref, kaug_ref, vaug_ref, m_ref, acc_ref,
                         sa_ref, sb_ref, *, fox, bq, bk):
    i = pl.program_id(2)

    @pl.when(i == 0)
    def _():
        kaug_ref[:, :HEAD_DIM] = k_ref[0]
        kaug_ref[:, HEAD_DIM:] = kx_ref[0, 0]
        _fill_v_aug(vaug_ref, v_ref)

    q = q_ref[0]
    if fox:
        lane = lax.broadcasted_iota(jnp.int32, (bq, HEAD_DIM), 1)
        q = jnp.concatenate([q, jnp.where(lane < 3, 1.0, 0.0).astype(BF16)], axis=1)
    _init_stats(m_ref, acc_ref)

    def logits(s_ref, j, row_start=0):
        s_ref[row_start:, :] = _qk(q[row_start:], _rows(kaug_ref, j, bk))

    def update(s_ref, j, **kw):
        _softmax_update(s_ref, _rows(vaug_ref, j, bk), m_ref, acc_ref, **kw)

    logits(sa_ref, 0)

    def pair(t, c):
        j = 2 * t
        logits(sb_ref, j + 1)
        update(sa_ref, j)
        logits(sa_ref, j + 2)
        update(sb_ref, j + 1)
        return c

    lax.fori_loop(0, i, pair, 0)
    logits(sb_ref, 2 * i + 1, row_start=bk)
    update(sa_ref, 2 * i, mask_off=0)
    update(sb_ref, 2 * i + 1, mask_off=bk, row_start=bk)
    o_ref[0] = _rms(_normalised(acc_ref), g_ref[0]).astype(o_ref.dtype)


def _diff_attn_kernel(q_ref, k_ref, v_ref, g_ref, bdiag_ref, bsub_ref, far_ref, lam_ref, _, o_ref,
                      vaug_ref, m1_ref, a1_ref, m2_ref, a2_ref, s1a_ref, s2a_ref, s1b_ref, s2b_ref,
                      *, bq, bk, out_scale):
    i = pl.program_id(2)

    @pl.when(i == 0)
    def _():
        _fill_v_aug(vaug_ref, v_ref)

    q = q_ref[0]
    half = lax.broadcasted_iota(jnp.int32, q.shape, 1) < (HEAD_DIM // 2)
    zero = jnp.zeros_like(q)
    q1 = jnp.where(half, q, zero)
    q2 = jnp.where(half, zero, q)
    _init_stats(m1_ref, a1_ref)
    _init_stats(m2_ref, a2_ref)
    k_ref = k_ref.at[0]
    buf_a = (s1a_ref, s2a_ref)
    buf_b = (s1b_ref, s2b_ref)

    def logits(buf, j, row_start=0):
        k = _rows(k_ref, j, bk)
        buf[0][row_start:, :] = _qk(q1[row_start:], k)
        buf[1][row_start:, :] = _qk(q2[row_start:], k)

    far = _lanes(far_ref[0], bk)

    def update(buf, j, bias_top, bias_bottom, **kw):
        v = _rows(vaug_ref, j, bk)
        bias = None if bias_top is None and bias_bottom is None else (bias_top, bias_bottom)
        for s_ref, m_ref, a_ref in ((buf[0], m1_ref, a1_ref), (buf[1], m2_ref, a2_ref)):
            _softmax_update(s_ref, v, m_ref, a_ref, bias=bias, **kw)

    logits(buf_a, 0)

    def pair(t, c):
        j = 2 * t
        logits(buf_b, j + 1)
        update(buf_a, j, None, None)
        logits(buf_a, j + 2)
        update(buf_b, j + 1, None, None)
        return c

    lax.fori_loop(0, i - 1, pair, 0)
    m1_ref[...] = m1_ref[...] + far_ref[0]
    m2_ref[...] = m2_ref[...] + far_ref[0]

    @pl.when(i >= 1)
    def _():
        logits(buf_b, 2 * i - 1)
        update(buf_a, 2 * i - 2, far, far)
        logits(buf_a, 2 * i)
        update(buf_b, 2 * i - 1, bsub_ref[0], far)

    logits(buf_b, 2 * i + 1, row_start=bk)
    update(buf_a, 2 * i, bdiag_ref[0], bsub_ref[0], mask_off=0)
    update(buf_b, 2 * i + 1, None, bdiag_ref[0], mask_off=bk, row_start=bk)
    o = _normalised(a1_ref) - lam_ref[...] * _normalised(a2_ref)
    o_ref[0] = (_rms(o, g_ref[0]) * out_scale).astype(o_ref.dtype)


def _sb_attn_kernel(q_ref, k_ref, v_ref, g_ref, u_ref, _, o_ref, r_ref, acc_ref, za_ref, zb_ref, *, bq, bk):
    i = pl.program_id(2)
    q = q_ref[0]
    k_ref = k_ref.at[0]
    v_ref = v_ref.at[0]
    r_ref[...] = jnp.zeros_like(r_ref)
    acc_ref[...] = jnp.zeros_like(acc_ref)
    u2 = u_ref[...]
    n_groups = bk // SB_GROUP
    sign_bit = jnp.uint32(0x80000000)

    def logits(z_ref, j):
        z_ref[...] = _qk(q, _rows(k_ref, jnp.maximum(j, 0), bk))

    def step(z_ref, j, mask_off=None):
        z = z_ref[...]
        r = r_ref[...]
        parts = [None] * n_groups
        for g in reversed(range(n_groups)):
            zg = z[:, g * SB_GROUP:(g + 1) * SB_GROUP]
            neg_abs = lax.bitcast_convert_type(lax.bitcast_convert_type(zg, jnp.uint32) | sign_bit, F32)
            sp = jnp.maximum(zg, 0.0) + jnp.log2(1.0 + jnp.exp2(neg_abs))
            if mask_off is not None:
                strict = _causal_mask(bq, SB_GROUP, mask_off + g * SB_GROUP, strict=True)
                sp = jnp.where(strict, sp, 0.0)
            hi = sp.astype(BF16)
            lo = (sp - hi.astype(F32)).astype(BF16)
            between = jnp.dot(jnp.concatenate([hi, lo], axis=1), u2, preferred_element_type=F32)
            a = jnp.exp2((zg - sp) + between + _lanes(r, SB_GROUP))
            if mask_off is not None:
                a = jnp.where(strict, a, 0.0)
            parts[g] = a.astype(BF16)
            r = r - jnp.sum(sp, axis=-1, keepdims=True)
        r_ref[...] = r
        acc_ref[...] += jnp.dot(jnp.concatenate(parts, axis=1), _rows(v_ref, j, bk),
                                preferred_element_type=F32)

    logits(za_ref, 2 * i + 1)
    logits(zb_ref, 2 * i)
    step(za_ref, 2 * i + 1, mask_off=bk)
    logits(za_ref, 2 * i - 1)
    step(zb_ref, 2 * i, mask_off=0)

    def pair(t, c):
        j = 2 * i - 1 - 2 * t
        logits(zb_ref, j - 1)
        step(za_ref, j)
        logits(za_ref, j - 2)
        step(zb_ref, j - 1)
        return c

    lax.fori_loop(0, i, pair, 0)

    o_ref[0] = _rms(acc_ref[...], g_ref[0]).astype(o_ref.dtype)


def _attn_call(kernel, o_buf, head_off, n_heads, q_arr, k_arr, v_arr, gain, extra_in, extra_specs, scratch,
               dq, q_col, k_col, v_col, name):
    nb, seq, _ = q_arr.shape
    bq, bk = _attn_blocks(seq)
    in_specs = [pl.BlockSpec((1, bq, dq), lambda b, h, i: (b, i, q_col(h))),
                pl.BlockSpec((1, seq, HEAD_DIM), lambda b, h, i: (b, 0, k_col(h))),
                pl.BlockSpec((1, seq, HEAD_DIM), lambda b, h, i: (b, 0, v_col(h))),
                pl.BlockSpec((1, 1, HEAD_DIM), lambda b, h, i: (h, 0, 0))]
    in_specs += extra_specs + [pl.BlockSpec(memory_space=pl.ANY)]
    n_in = len(in_specs)
    return pl.pallas_call(
        functools.partial(kernel, bq=bq, bk=bk),
        out_shape=jax.ShapeDtypeStruct(o_buf.shape, o_buf.dtype),
        grid=(nb, n_heads, seq // bq),
        in_specs=in_specs,
        out_specs=pl.BlockSpec((1, bq, HEAD_DIM), lambda b, h, i: (b, i, head_off + h)),
        scratch_shapes=scratch,
        input_output_aliases={n_in - 1: 0},
        compiler_params=_params("arbitrary", "arbitrary", "arbitrary", flags=ATTN_FLAGS), name=name,
    )(q_arr, k_arr, v_arr, gain.reshape(n_heads, 1, HEAD_DIM).astype(F32), *extra_in, o_buf)


def _attn_blocks(seq):
    bk = min(ATTN_KEY_BLOCK, seq // 2)
    assert seq % (2 * bk) == 0 and bk % SB_GROUP == 0
    return 2 * bk, bk


def _stat_scratch(bq, n):
    out = []
    for _ in range(n):
        out += [pltpu.VMEM((bq, HEAD_DIM), F32), pltpu.VMEM((bq, 2 * HEAD_DIM), F32)]
    return out


def _logit_scratch(bq, bk, n):
    return [pltpu.VMEM((bq, bk), F32)] * n


def _group(off):
    return lambda h: off + h


def sb_attention(o_buf, head_off, qkv, gain, q_off, k_off, v_off):
    bq, bk = _attn_blocks(qkv.shape[1])
    u = -jnp.tril(jnp.ones((SB_GROUP, SB_GROUP), F32), -1).astype(BF16)
    u2 = jnp.concatenate([u, u], axis=0)
    return _attn_call(_sb_attn_kernel, o_buf, head_off, N_SB, qkv, qkv, qkv, gain, [u2],
                      [pl.BlockSpec((2 * SB_GROUP, SB_GROUP), lambda b, h, i: (0, 0))],
                      [pltpu.VMEM((bq, HEAD_DIM), F32), pltpu.VMEM((bq, HEAD_DIM), F32)]
                      + _logit_scratch(bq, bk, 2),
                      HEAD_DIM, _group(q_off), _group(k_off), _group(v_off), "sb_attention")


def _toeplitz(g, n):
    nh = g.shape[0]
    t = jnp.broadcast_to(g[:, None, :], (nh, n, 2 * n - 1))
    t = jnp.pad(t, ((0, 0), (0, 0), (0, 1))).reshape(nh, 2 * n * n)[:, :n * (2 * n - 1)]
    return t.reshape(nh, n, 2 * n - 1)[:, :, n - 1:]


def diff_attention(o_buf, head_off, qkv, gain, t5_table, lam, out_scale, q_off, k_off, v_off):
    seq = qkv.shape[1]
    bq, bk = _attn_blocks(seq)
    assert bk >= MAX_DISTANCE
    table = t5_table.astype(F32) * LOG2E
    f_table = table[_t5_bucket(jnp.arange(2 * bk, dtype=jnp.int32))].T
    x = jnp.arange(2 * bk - 1)
    b_diag = _toeplitz(f_table[:, jnp.maximum(bk - 1 - x, 0)], bk)
    b_sub = _toeplitz(f_table[:, 2 * bk - 1 - x], bk)
    far = jnp.broadcast_to(table[N_BUCKETS - 1][:, None, None], (N_DIFF, 1, HEAD_DIM))
    lam_row = jnp.broadcast_to(lam.astype(F32).reshape(1, 1), (1, HEAD_DIM))
    blk = pl.BlockSpec((1, bk, bk), lambda b, h, i: (h, 0, 0))
    return _attn_call(functools.partial(_diff_attn_kernel, out_scale=out_scale), o_buf, head_off, N_DIFF,
                      qkv, qkv, qkv, gain, [b_diag, b_sub, far, lam_row],
                      [blk, blk, pl.BlockSpec((1, 1, HEAD_DIM), lambda b, h, i: (h, 0, 0)),
                       pl.BlockSpec((1, HEAD_DIM), lambda b, h, i: (0, 0))],
                      [pltpu.VMEM((seq, 2 * HEAD_DIM), BF16)] + _stat_scratch(bq, 2)
                      + _logit_scratch(bq, bk, 4),
                      HEAD_DIM, _group(q_off), _group(k_off), _group(v_off), "diff_attention")


def _wide_softmax_attention(o_buf, head_off, fox, n_heads, q_arr, dq, q_col, kv_arr, k_col, v_col, kx, kx_spec,
                            gain, name):
    seq = q_arr.shape[1]
    bq, bk = _attn_blocks(seq)
    scratch = [pltpu.VMEM((seq, 2 * HEAD_DIM), BF16), pltpu.VMEM((seq, 2 * HEAD_DIM), BF16)]
    return _attn_call(functools.partial(_softmax_attn_kernel, fox=fox), o_buf, head_off, n_heads,
                      q_arr, kv_arr, kv_arr, gain, [kx], [kx_spec],
                      scratch + _stat_scratch(bq, 1) + _logit_scratch(bq, bk, 2),
                      dq, q_col, k_col, v_col, name)


def mla_attention(o_buf, head_off, q_cat, kv, k_rope_pad, gain):
    nb, seq, _ = q_cat.shape
    spec = pl.BlockSpec((1, 1, seq, HEAD_DIM), lambda b, h, i: (b, 0, 0, 0))
    return _wide_softmax_attention(o_buf, head_off, False, N_MLA, q_cat, MLA_QK_PAD, lambda h: h,
                                   kv, lambda h: 2 * h, lambda h: 2 * h + 1,
                                   k_rope_pad.reshape(nb, 1, seq, HEAD_DIM), spec, gain, "mla_attention")


def fox_attention(o_buf, head_off, qkv, gain, c, q_off, k_off, v_off):
    nb, seq, _ = qkv.shape
    c2 = jnp.moveaxis(c, -1, 1) * LOG2E
    hi = c2.astype(BF16)
    mid = (c2 - hi.astype(F32)).astype(BF16)
    lo = (c2 - hi.astype(F32) - mid.astype(F32)).astype(BF16)
    kx = jnp.stack([-hi, -mid, -lo], axis=-1)
    kx = jnp.pad(kx, ((0, 0), (0, 0), (0, 0), (0, HEAD_DIM - 3)))
    spec = pl.BlockSpec((1, 1, seq, HEAD_DIM), lambda b, h, i: (b, h, 0, 0))
    return _wide_softmax_attention(o_buf, head_off, True, N_FOX, qkv, HEAD_DIM, _group(q_off),
                                   qkv, _group(k_off), _group(v_off), kx, spec, gain, "fox_attention")


def _xattn_kernel(q_ref, k_ref, v_ref, o_ref):
    for h in range(N_XHEADS):
        sl = slice(h * HEAD_DIM, (h + 1) * HEAD_DIM)
        s = _qk(q_ref[0, :, sl], k_ref[0, :, sl])
        p = jnp.exp(s - jnp.max(s, axis=-1, keepdims=True))
        o = jnp.dot(p.astype(BF16), v_ref[0, :, sl], preferred_element_type=F32)
        o_ref[0, :, sl] = (o / jnp.sum(p, axis=-1, keepdims=True)).astype(o_ref.dtype)


def cross_attention(q, k, v, bq=512):
    nb, seq, w = q.shape
    n_mem = k.shape[1]
    bq = min(bq, seq)
    return pl.pallas_call(
        _xattn_kernel,
        out_shape=jax.ShapeDtypeStruct((nb, seq, w), BF16),
        grid=(nb, seq // bq),
        in_specs=[pl.BlockSpec((1, bq, w), lambda b, i: (b, i, 0)),
                  pl.BlockSpec((1, n_mem, w), lambda b, i: (b, 0, 0)),
                  pl.BlockSpec((1, n_mem, w), lambda b, i: (b, 0, 0))],
        out_specs=pl.BlockSpec((1, bq, w), lambda b, i: (b, i, 0)),
        compiler_params=_params("parallel", "parallel"), name="cross_attention",
    )(q, k, v)


def _t5_bucket(rel):
    n = jnp.maximum(rel, 0)
    max_exact = N_BUCKETS // 2
    nf = jnp.maximum(n, 1).astype(F32)
    large = max_exact + (jnp.log(nf / max_exact) / math.log(MAX_DISTANCE / max_exact)
                         * (N_BUCKETS - max_exact)).astype(jnp.int32)
    large = jnp.minimum(large, N_BUCKETS - 1)
    return jnp.where(n < max_exact, n, large)


def _rope(x, cos, sin):
    x1, x2 = jnp.split(x, 2, axis=-1)
    return jnp.concatenate([x1 * cos - x2 * sin, x2 * cos + x1 * sin], axis=-1)


def _mixer(h, nb, seq, cos, sin, t5_table, w_in, q_norm, w_uq, kv_norm, w_ukv, lam_vecs, b_forget,
           head_norm, w_out, layer_idx):
    d = h.shape[1]
    hw = N_SB * HEAD_DIM
    o_df = 3 * hw
    o_cq = 6 * hw
    o_ckv = o_cq + MLA_Q_RANK
    o_kr = o_ckv + MLA_KV_RANK
    o_fx = o_kr + MLA_ROPE
    o_fl = o_fx + 3 * hw
    sc = HEAD_DIM ** -0.5 * LOG2E
    sc_df = (HEAD_DIM // 2) ** -0.5 * LOG2E
    sc_mla = (MLA_NOPE + MLA_ROPE) ** -0.5 * LOG2E
    col_scale = jnp.ones((w_in.shape[1],), F32)
    col_scale = col_scale.at[0:hw].set(sc).at[o_df:o_df + hw].set(sc_df).at[o_fx:o_fx + hw].set(sc)
    w_bf = (w_in * col_scale).astype(BF16)
    w_main = jnp.concatenate([w_bf[:, :o_cq], w_bf[:, o_fx:o_fl]], axis=1)
    o_s_ckv = 2 * MLA_KV_RANK
    o_s_kr = o_s_ckv + MLA_KV_RANK
    n_small = o_s_kr + MLA_ROPE + N_FOX
    n_small_pad = n_small + (-n_small) % 128
    w_small = jnp.concatenate([w_bf[:, o_cq:o_ckv], jnp.zeros((d, o_s_ckv - MLA_Q_RANK), BF16),
                               w_bf[:, o_ckv:o_fx], w_bf[:, o_fl:o_fl + N_FOX],
                               jnp.zeros((d, n_small_pad - n_small), BF16)], axis=1)
    qkv = matmul(h, w_main, BF16).reshape(nb, seq, 9 * hw)
    small = matmul(h, w_small, F32, bm=512, bn=n_small_pad)
    k_r = small[:, o_s_kr:o_s_kr + MLA_ROPE]
    f_logit = small[:, o_s_kr + MLA_ROPE:n_small]

    g_sb, g_df, g_mla, g_fx = jnp.split(head_norm, [N_SB, N_SB + N_DIFF, N_SB + N_DIFF + N_MLA], axis=0)
    o = jnp.zeros((nb, seq, (N_SB + N_DIFF + N_MLA + N_FOX) * HEAD_DIM), BF16)

    o = sb_attention(o, 0, qkv, g_sb, 0, N_SB, 2 * N_SB)

    lv = lam_vecs.astype(F32)
    lam_init = 0.8 - 0.6 * math.exp(-0.3 * layer_idx)
    lam = jnp.exp(jnp.sum(lv[0] * lv[1])) - jnp.exp(jnp.sum(lv[2] * lv[3])) + lam_init
    o = diff_attention(o, N_SB, qkv, g_df, t5_table, lam, 1.0 - lam_init, 3 * N_SB, 4 * N_SB, 5 * N_SB)

    wq = (w_uq * sc_mla).reshape(MLA_Q_RANK, N_MLA, MLA_NOPE + MLA_ROPE)
    w_r = wq[..., MLA_NOPE:]
    w_rot = jnp.concatenate([w_r[..., MLA_ROPE // 2:], w_r[..., :MLA_ROPE // 2]], axis=-1)
    wq = jnp.concatenate([wq, w_rot], axis=-1).reshape(MLA_Q_RANK, N_MLA * MLA_QK_PAD).astype(BF16)
    zeros = jnp.zeros_like(cos)
    cs = jnp.concatenate([cos, cos, zeros, zeros], axis=-1).reshape(nb * seq, HEAD_DIM)
    sn = jnp.concatenate([-sin, sin, zeros, zeros], axis=-1).reshape(nb * seq, HEAD_DIM)
    q_cat = norm_matmul(small, 0, MLA_Q_RANK, q_norm, wq, rope=(cs, sn, N_MLA))
    q_cat = q_cat.reshape(nb, seq, N_MLA * MLA_QK_PAD)
    kv = norm_matmul(small, o_s_ckv // MLA_KV_RANK, MLA_KV_RANK, kv_norm, w_ukv.astype(BF16))
    kv = kv.reshape(nb, seq, N_MLA * 2 * HEAD_DIM)
    k_rope = _rope(k_r.reshape(nb, seq, MLA_ROPE), cos, sin)
    k_rope_pad = jnp.pad(k_rope, ((0, 0), (0, 0), (0, HEAD_DIM - MLA_ROPE))).astype(BF16)
    o = mla_attention(o, N_SB + N_DIFF, q_cat, kv, k_rope_pad, g_mla)

    log_f = jax.nn.log_sigmoid(f_logit.reshape(nb, seq, N_FOX) + b_forget.astype(F32))
    c = lax.cumsum(log_f, axis=1)
    o = fox_attention(o, N_SB + N_DIFF + N_MLA, qkv, g_fx, c, 6 * N_SB, 7 * N_SB, 8 * N_SB)

    return matmul(o.reshape(nb * seq, -1), w_out.astype(BF16), F32)


def kernel(x, mem, positions, t5_table, mix_norm_pre, w_in, mla_q_norm, w_uq, mla_kv_norm, w_ukv, diff_lambda, b_forget, head_norm, w_out, mix_norm_post, xattn_norm_pre, mem_norm, w_xq, w_xk, w_xv, w_xo, xattn_norm_post, ffn_norm_pre, w_gate, w_up, conv_w, conv_b, w_down, ffn_norm_post):
    nb, seq, d = x.shape
    depth = w_in.shape[0]
    n_mem = mem.shape[1]
    d_ff = w_gate.shape[2]
    xw = N_XHEADS * HEAD_DIM
    inv = ROPE_THETA ** (-jnp.arange(0, MLA_ROPE, 2, dtype=F32) / MLA_ROPE)
    ang = positions.astype(F32)[..., None] * inv
    cos, sin = jnp.cos(ang), jnp.sin(ang)

    xf = x.reshape(nb * seq, d)
    memf = mem.reshape(nb * n_mem, d)
    h = rmsnorm(xf, mix_norm_pre[0])
    for l in range(depth):
        y = _mixer(h, nb, seq, cos, sin, t5_table, w_in[l], mla_q_norm[l], w_uq[l], mla_kv_norm[l],
                   w_ukv[l], diff_lambda[l], b_forget[l], head_norm[l], w_out[l], l)
        xf, h = addnorm(y, xf, mix_norm_post[l], xattn_norm_pre[l])

        mem_n = rmsnorm(memf, mem_norm[l])
        q = matmul(h, (w_xq[l] * HEAD_DIM ** -0.5).astype(BF16), BF16).reshape(nb, seq, xw)
        k = matmul(mem_n, w_xk[l].astype(BF16), BF16).reshape(nb, n_mem, xw)
        v = matmul(mem_n, w_xv[l].astype(BF16), BF16).reshape(nb, n_mem, xw)
        o = cross_attention(q, k, v).reshape(nb * seq, xw)
        xf, h = matmul_addnorm(o, w_xo[l].astype(BF16), xf, xattn_norm_post[l], ffn_norm_pre[l])

        act = ffn_up(h, w_gate[l].astype(BF16), w_up[l].astype(BF16), conv_w[l], conv_b[l], seq)
        y = matmul(act, w_down[l].astype(BF16), F32, bm=512, bn=512)
        g_next = mix_norm_pre[l + 1] if l + 1 < depth else None
        xf, h = addnorm(y, xf, ffn_norm_post[l], g_next)
    return xf.reshape(nb, seq, d)
```

```python
import functools
import math

import jax
import jax.numpy as jnp
from jax import lax
from jax.experimental import pallas as pl
from jax.experimental.pallas import tpu as pltpu

F32 = jnp.float32
BF16 = jnp.bfloat16

HEAD_DIM = 128
N_SB = 8
N_DIFF = 8
N_MLA = 8
N_FOX = 8
MLA_Q_RANK = 768
MLA_KV_RANK = 512
MLA_NOPE = 128
MLA_ROPE = 64
MLA_QK_PAD = 256
N_XHEADS = 4
N_BUCKETS = 32
MAX_DISTANCE = 128
ROPE_THETA = 10000.0
EPS = 1e-6
NEG_BIG = -1e30
LOG2E = math.log2(math.e)

V7X_VMEM_LIMIT_BYTES = 56 * 1024 * 1024
ATTN_KEY_BLOCK = 512
SB_GROUP = 256
SB_LOGIT_CAP = 120.0
FFN_CHUNK = 256


def _params(*sem, flags=None):
    return pltpu.CompilerParams(dimension_semantics=sem, vmem_limit_bytes=V7X_VMEM_LIMIT_BYTES, flags=flags)


ATTN_FLAGS = None


def _rms(x, g):
    return x * lax.rsqrt(jnp.mean(x * x, axis=-1, keepdims=True) + EPS) * g


def _rmsnorm_kernel(x_ref, g_ref, o_ref):
    o_ref[...] = _rms(x_ref[...].astype(F32), g_ref[...]).astype(o_ref.dtype)


def rmsnorm(x, g, out_dtype=BF16, bm=512):
    m, d = x.shape
    bm = min(bm, m)
    return pl.pallas_call(
        _rmsnorm_kernel,
        out_shape=jax.ShapeDtypeStruct((m, d), out_dtype),
        grid=(pl.cdiv(m, bm),),
        in_specs=[pl.BlockSpec((bm, d), lambda i: (i, 0)), pl.BlockSpec((1, d), lambda i: (0, 0))],
        out_specs=pl.BlockSpec((bm, d), lambda i: (i, 0)),
        compiler_params=_params("parallel"),
        name="rmsnorm",
    )(x, g.reshape(1, d).astype(F32))


def _addnorm_kernel(y_ref, x_ref, gpost_ref, gnext_ref, xo_ref, ho_ref):
    xn = x_ref[...] + _rms(y_ref[...], gpost_ref[...])
    xo_ref[...] = xn
    ho_ref[...] = _rms(xn, gnext_ref[...]).astype(ho_ref.dtype)


def _addnorm_last_kernel(y_ref, x_ref, gpost_ref, xo_ref):
    xo_ref[...] = x_ref[...] + _rms(y_ref[...], gpost_ref[...])


def addnorm(y, x, g_post, g_next=None, bm=256):
    m, d = x.shape
    bm = min(bm, m)
    row = pl.BlockSpec((bm, d), lambda i: (i, 0))
    vec = pl.BlockSpec((1, d), lambda i: (0, 0))
    if g_next is None:
        return pl.pallas_call(
            _addnorm_last_kernel,
            out_shape=jax.ShapeDtypeStruct((m, d), F32),
            grid=(pl.cdiv(m, bm),), in_specs=[row, row, vec], out_specs=row,
            compiler_params=_params("parallel"), name="addnorm_last",
        )(y, x, g_post.reshape(1, d)), None
    return pl.pallas_call(
        _addnorm_kernel,
        out_shape=(jax.ShapeDtypeStruct((m, d), F32), jax.ShapeDtypeStruct((m, d), BF16)),
        grid=(pl.cdiv(m, bm),), in_specs=[row, row, vec, vec], out_specs=(row, row),
        compiler_params=_params("parallel"), name="addnorm",
    )(y, x, g_post.reshape(1, d), g_next.reshape(1, d))


def _matmul_kernel(a_ref, b_ref, o_ref):
    o_ref[...] = jnp.dot(a_ref[...], b_ref[...], preferred_element_type=F32).astype(o_ref.dtype)


def _matmul_acc_kernel(a_ref, b_ref, o_ref, acc_ref, *, nk):
    k = pl.program_id(2)

    @pl.when(k == 0)
    def _():
        acc_ref[...] = jnp.zeros_like(acc_ref)

    acc_ref[...] += jnp.dot(a_ref[...], b_ref[...], preferred_element_type=F32)

    @pl.when(k == nk - 1)
    def _():
        o_ref[...] = acc_ref[...].astype(o_ref.dtype)


def _pick_block(n, target):
    if n <= target:
        return n
    best = 128
    for cand in range(128, target + 1, 128):
        if n % cand == 0:
            best = cand
    assert n % best == 0
    return best


def _wspec(w, layer, block, index_map):
    if w.ndim == 2:
        return pl.BlockSpec(block, index_map)
    return pl.BlockSpec((None,) + tuple(block), lambda *g: (layer,) + tuple(index_map(*g)))


def matmul(a, b, out_dtype, bm=1024, bn=1024, bk=None, layer=None):
    m, kdim = a.shape
    n = b.shape[-1]
    bm, bn = _pick_block(m, bm), _pick_block(n, bn)
    bk = kdim if bk is None else bk
    assert kdim % bk == 0
    nk = kdim // bk
    if nk == 1:
        return pl.pallas_call(
            _matmul_kernel,
            out_shape=jax.ShapeDtypeStruct((m, n), out_dtype),
            grid=(pl.cdiv(m, bm), pl.cdiv(n, bn)),
            in_specs=[pl.BlockSpec((bm, kdim), lambda i, j: (i, 0)),
                      _wspec(b, layer, (kdim, bn), lambda i, j: (0, j))],
            out_specs=pl.BlockSpec((bm, bn), lambda i, j: (i, j)),
            compiler_params=_params("parallel", "parallel"), name="matmul",
        )(a, b)
    return pl.pallas_call(
        functools.partial(_matmul_acc_kernel, nk=nk),
        out_shape=jax.ShapeDtypeStruct((m, n), out_dtype),
        grid=(pl.cdiv(m, bm), pl.cdiv(n, bn), nk),
        in_specs=[pl.BlockSpec((bm, bk), lambda i, j, k: (i, k)),
                  _wspec(b, layer, (bk, bn), lambda i, j, k: (k, j))],
        out_specs=pl.BlockSpec((bm, bn), lambda i, j, k: (i, j)),
        scratch_shapes=[pltpu.VMEM((bm, bn), F32)],
        compiler_params=_params("parallel", "parallel", "arbitrary"), name="matmul_acc",
    )(a, b)


def _matmul_addnorm_kernel(a_ref, w_ref, x_ref, gpost_ref, gnext_ref, xo_ref, ho_ref):
    y = jnp.dot(a_ref[...], w_ref[...], preferred_element_type=F32)
    xn = x_ref[...] + _rms(y, gpost_ref[...])
    xo_ref[...] = xn
    ho_ref[...] = _rms(xn, gnext_ref[...]).astype(ho_ref.dtype)


def matmul_addnorm(a, w, x, g_post, g_next, bm=256, layer=None):
    m, kdim = a.shape
    d = w.shape[-1]
    bm = _pick_block(m, bm)
    row = pl.BlockSpec((bm, d), lambda i: (i, 0))
    vec = pl.BlockSpec((1, d), lambda i: (0, 0))
    return pl.pallas_call(
        _matmul_addnorm_kernel,
        out_shape=(jax.ShapeDtypeStruct((m, d), F32), jax.ShapeDtypeStruct((m, d), BF16)),
        grid=(m // bm,),
        in_specs=[pl.BlockSpec((bm, kdim), lambda i: (i, 0)), _wspec(w, layer, (kdim, d), lambda i: (0, 0)),
                  row, vec, vec],
        out_specs=(row, row),
        compiler_params=_params("parallel"), name="matmul_addnorm",
    )(a, w, x, g_post.reshape(1, d), g_next.reshape(1, d))


def _norm_matmul_kernel(x_ref, g_ref, w_ref, *rest, rope_heads):
    o_ref = rest[-1]
    xn = _rms(x_ref[...], g_ref[...]).astype(BF16)
    y = jnp.dot(xn, w_ref[...], preferred_element_type=F32)
    if not rope_heads:
        o_ref[...] = y.astype(o_ref.dtype)
        return
    cs, sn = rest[0][...], rest[1][...]
    for hd in range(rope_heads):
        c0 = hd * MLA_QK_PAD
        x = y[:, c0 + MLA_NOPE:c0 + MLA_QK_PAD]
        o_ref[:, c0:c0 + MLA_NOPE] = y[:, c0:c0 + MLA_NOPE].astype(o_ref.dtype)
        o_ref[:, c0 + MLA_NOPE:c0 + MLA_QK_PAD] = (x * cs + pltpu.roll(x, MLA_ROPE, axis=1) * sn).astype(o_ref.dtype)


def norm_matmul(x_arr, col_block, kdim, g, w, rope=None, bm=512, layer=None):
    m = x_arr.shape[0]
    n = w.shape[-1]
    bm = _pick_block(m, bm)
    in_specs = [pl.BlockSpec((bm, kdim), lambda i: (i, col_block)),
                pl.BlockSpec((1, kdim), lambda i: (0, 0)),
                _wspec(w, layer, (kdim, n), lambda i: (0, 0))]
    args = [x_arr, g.reshape(1, kdim).astype(F32), w]
    if rope is not None:
        in_specs += [pl.BlockSpec((bm, HEAD_DIM), lambda i: (i, 0))] * 2
        args += [rope[0], rope[1]]
    return pl.pallas_call(
        functools.partial(_norm_matmul_kernel, rope_heads=0 if rope is None else rope[2]),
        out_shape=jax.ShapeDtypeStruct((m, n), BF16),
        grid=(m // bm,), in_specs=in_specs,
        out_specs=pl.BlockSpec((bm, n), lambda i: (i, 0)),
        compiler_params=_params("parallel"), name="norm_matmul",
    )(*args)


def _ffn_up_kernel(h_ref, wg_ref, wu_ref, cw_ref, cb_ref, *rest, tiles_per_seq):
    o_ref, carry_ref = rest[-2:]
    i = pl.program_id(1)
    bm = h_ref.shape[0]
    bn = o_ref.shape[1]

    @pl.when(i % tiles_per_seq == 0)
    def _():
        carry_ref[...] = jnp.zeros_like(carry_ref)

    row8 = lax.broadcasted_iota(jnp.int32, (8, 1), 0)
    for c0 in range(0, bn, FFN_CHUNK):
        cols = slice(c0, c0 + FFN_CHUNK)
        gate = jnp.dot(h_ref[...], wg_ref[:, cols], preferred_element_type=F32)
        up = jnp.dot(h_ref[...], wu_ref[:, cols], preferred_element_type=F32)
        prev = carry_ref[:, cols]
        carry_ref[:, cols] = gate[bm - 8:, :]
        g1 = pltpu.roll(gate, 1, axis=0)
        g2 = pltpu.roll(gate, 2, axis=0)
        p1 = pltpu.roll(prev, 1, axis=0)
        p2 = pltpu.roll(prev, 2, axis=0)
        g1 = jnp.concatenate([jnp.where(row8 < 1, p1, g1[:8]), g1[8:]], axis=0)
        g2 = jnp.concatenate([jnp.where(row8 < 2, p2, g2[:8]), g2[8:]], axis=0)
        cw = cw_ref[:, cols]
        conv = cw[0:1] * g2 + cw[1:2] * g1 + cw[2:3] * gate + cb_ref[:, cols]
        o_ref[:, cols] = (jax.nn.gelu(conv, approximate=True) * up).astype(o_ref.dtype)


def _ffn_up_call(h, wg, wu, conv_w, conv_b, seq, out, col0, n_cols, bm, bn, layer):
    m, d = h.shape
    f = wg.shape[-1]
    assert n_cols % bn == 0 and col0 % bn == 0 and bn % FFN_CHUNK == 0
    jb = col0 // bn
    in_specs = [pl.BlockSpec((bm, d), lambda j, i: (i, 0)),
                _wspec(wg, layer, (d, bn), lambda j, i: (0, jb + j)),
                _wspec(wu, layer, (d, bn), lambda j, i: (0, jb + j)),
                pl.BlockSpec((3, bn), lambda j, i: (0, jb + j)),
                pl.BlockSpec((1, bn), lambda j, i: (0, jb + j))]
    args = [h, wg, wu, conv_w, conv_b]
    aliases = {}
    if out is not None:
        in_specs.append(pl.BlockSpec(memory_space=pl.ANY))
        args.append(out)
        aliases = {5: 0}
    return pl.pallas_call(
        functools.partial(_ffn_up_kernel, tiles_per_seq=seq // bm),
        out_shape=jax.ShapeDtypeStruct((m, f), BF16),
        grid=(n_cols // bn, m // bm),
        in_specs=in_specs,
        out_specs=pl.BlockSpec((bm, bn), lambda j, i: (i, jb + j)),
        scratch_shapes=[pltpu.VMEM((8, bn), F32)],
        input_output_aliases=aliases,
        compiler_params=_params("arbitrary", "arbitrary"), name="ffn_up",
    )(*args)


def ffn_up(h, wg, wu, conv_w, conv_b, seq, bm=1024, bn=512, layer=None):
    m, d = h.shape
    f = wg.shape[-1]
    bm = _pick_block(seq, bm)
    assert seq % bm == 0 and m % seq == 0 and bm % 8 == 0 and f % FFN_CHUNK == 0
    conv_w = conv_w.astype(F32)
    conv_b = conv_b.reshape(1, f).astype(F32)
    bn = min(bn, f)
    main = (f // bn) * bn
    out = _ffn_up_call(h, wg, wu, conv_w, conv_b, seq, None, 0, main, bm, bn, layer)
    if main < f:
        out = _ffn_up_call(h, wg, wu, conv_w, conv_b, seq, out, main, f - main, bm,
                           _pick_block(f - main, bn), layer)
    return out


def _qk(q, k):
    return lax.dot_general(q, k, (((1,), (1,)), ((), ())), preferred_element_type=F32)


def _lanes(x, n):
    return x if n == HEAD_DIM else jnp.concatenate([x] * (n // HEAD_DIM), axis=1)


def _rows(ref, j, bk):
    return ref[pl.ds(pl.multiple_of(j * bk, bk), bk), :]


def _causal_mask(bq, bk, col_off, strict=False):
    row = lax.broadcasted_iota(jnp.int32, (bq, bk), 0)
    col = lax.broadcasted_iota(jnp.int32, (bq, bk), 1) + col_off
    return col < row if strict else col <= row


def _softmax_update(s_ref, v_aug, m_ref, acc_ref, bias=None, mask_off=None, row_start=0):
    bq, bk = s_ref.shape
    live = slice(row_start, bq)
    if bias is None:
        s = s_ref[live, :]
    else:
        s = jnp.concatenate([s_ref[r0:r0 + bk, :] + bias[r0 // bk] for r0 in range(row_start, bq, bk)], axis=0)
    if mask_off is not None:
        row = lax.broadcasted_iota(jnp.int32, s.shape, 0) + row_start
        col = lax.broadcasted_iota(jnp.int32, s.shape, 1) + mask_off
        s = jnp.where(col <= row, s, NEG_BIG)
    m_prev = m_ref[live, :]
    m_new = jnp.maximum(m_prev, jnp.max(s, axis=-1, keepdims=True))
    alpha = jnp.exp2(m_prev - m_new)
    p = jnp.exp2(s - _lanes(m_new, bk))
    pv = jnp.dot(p.astype(BF16), v_aug, preferred_element_type=F32)
    acc_ref[live, :] = _lanes(alpha, 2 * HEAD_DIM) * acc_ref[live, :] + pv
    m_ref[live, :] = m_new


def _init_stats(m_ref, acc_ref):
    m_ref[...] = jnp.full_like(m_ref, NEG_BIG)
    acc_ref[...] = jnp.zeros_like(acc_ref)


def _normalised(acc_ref):
    acc = acc_ref[...]
    return acc[:, :HEAD_DIM] / acc[:, HEAD_DIM:]


def _fill_v_aug(vaug_ref, v_ref):
    vaug_ref[:, :HEAD_DIM] = v_ref[0]
    vaug_ref[:, HEAD_DIM:] = jnp.ones((vaug_ref.shape[0], HEAD_DIM), BF16)


def _softmax_attn_kernel(q_ref, k_ref, v_ref, g_ref, kx_ref, _, o_ref, kaug_ref, vaug_ref, m_ref, acc_ref,
                         sa_ref, sb_ref, *, fox, bq, bk):
    i = pl.program_id(2)

    @pl.when(i == 0)
    def _():
        kaug_ref[:, :HEAD_DIM] = k_ref[0]
        kaug_ref[:, HEAD_DIM:] = kx_ref[0, 0]
        _fill_v_aug(vaug_ref, v_ref)

    q = q_ref[0]
    if fox:
        lane = lax.broadcasted_iota(jnp.int32, (bq, HEAD_DIM), 1)
        q = jnp.concatenate([q, jnp.where(lane < 3, 1.0, 0.0).astype(BF16)], axis=1)
    _init_stats(m_ref, acc_ref)

    def logits(s_ref, j, row_start=0):
        s_ref[row_start:, :] = _qk(q[row_start:], _rows(kaug_ref, j, bk))

    def update(s_ref, j, **kw):
        _softmax_update(s_ref, _rows(vaug_ref, j, bk), m_ref, acc_ref, **kw)

    logits(sa_ref, 0)

    def pair(t, c):
        j = 2 * t
        logits(sb_ref, j + 1)
        update(sa_ref, j)
        logits(sa_ref, j + 2)
        update(sb_ref, j + 1)
        return c

    lax.fori_loop(0, i, pair, 0)
    logits(sb_ref, 2 * i + 1, row_start=bk)
    update(sa_ref, 2 * i, mask_off=0)
    update(sb_ref, 2 * i + 1, mask_off=bk, row_start=bk)
    o_ref[0] = _rms(_normalised(acc_ref), g_ref[0]).astype(o_ref.dtype)


def _diff_attn_kernel(q_ref, k_ref, v_ref, g_ref, bdiag_ref, bsub_ref, far_ref, lam_ref, _, o_ref,
                      vaug_ref, m1_ref, a1_ref, m2_ref, a2_ref, s1a_ref, s2a_ref, s1b_ref, s2b_ref,
                      *, bq, bk, out_scale):
    i = pl.program_id(2)

    @pl.when(i == 0)
    def _():
        _fill_v_aug(vaug_ref, v_ref)

    q = q_ref[0]
    half = lax.broadcasted_iota(jnp.int32, q.shape, 1) < (HEAD_DIM // 2)
    zero = jnp.zeros_like(q)
    q1 = jnp.where(half, q, zero)
    q2 = jnp.where(half, zero, q)
    _init_stats(m1_ref, a1_ref)
    _init_stats(m2_ref, a2_ref)
    k_ref = k_ref.at[0]
    buf_a = (s1a_ref, s2a_ref)
    buf_b = (s1b_ref, s2b_ref)

    def logits(buf, j, row_start=0):
        k = _rows(k_ref, j, bk)
        buf[0][row_start:, :] = _qk(q1[row_start:], k)
        buf[1][row_start:, :] = _qk(q2[row_start:], k)

    far = _lanes(far_ref[0], bk)

    def update(buf, j, bias_top, bias_bottom, **kw):
        v = _rows(vaug_ref, j, bk)
        bias = None if bias_top is None and bias_bottom is None else (bias_top, bias_bottom)
        for s_ref, m_ref, a_ref in ((buf[0], m1_ref, a1_ref), (buf[1], m2_ref, a2_ref)):
            _softmax_update(s_ref, v, m_ref, a_ref, bias=bias, **kw)

    logits(buf_a, 0)

    def pair(t, c):
        j = 2 * t
        logits(buf_b, j + 1)
        update(buf_a, j, None, None)
        logits(buf_a, j + 2)
        update(buf_b, j + 1, None, None)
        return c

    lax.fori_loop(0, i - 1, pair, 0)
    m1_ref[...] = m1_ref[...] + far_ref[0]
    m2_ref[...] = m2_ref[...] + far_ref[0]

    @pl.when(i >= 1)
    def _():
        logits(buf_b, 2 * i - 1)
        update(buf_a, 2 * i - 2, far, far)
        logits(buf_a, 2 * i)
        update(buf_b, 2 * i - 1, bsub_ref[0], far)

    logits(buf_b, 2 * i + 1, row_start=bk)
    update(buf_a, 2 * i, bdiag_ref[0], bsub_ref[0], mask_off=0)
    update(buf_b, 2 * i + 1, None, bdiag_ref[0], mask_off=bk, row_start=bk)
    o = _normalised(a1_ref) - lam_ref[...] * _normalised(a2_ref)
    o_ref[0] = (_rms(o, g_ref[0]) * out_scale).astype(o_ref.dtype)


def _sb_attn_kernel(q_ref, k_ref, v_ref, g_ref, u_ref, _, o_ref, r_ref, acc_ref, za_ref, zb_ref, *, bq, bk):
    i = pl.program_id(2)
    q = q_ref[0]
    k_ref = k_ref.at[0]
    v_ref = v_ref.at[0]
    r_ref[...] = jnp.zeros_like(r_ref)
    acc_ref[...] = jnp.zeros_like(acc_ref)
    u2 = u_ref[...]
    n_groups = bk // SB_GROUP

    def logits(z_ref, j):
        z_ref[...] = _qk(q, _rows(k_ref, jnp.maximum(j, 0), bk))

    def step(z_ref, j, mask_off=None):
        z = z_ref[...]
        r = r_ref[...]
        parts = [None] * n_groups
        for g in reversed(range(n_groups)):
            zg = jnp.minimum(z[:, g * SB_GROUP:(g + 1) * SB_GROUP], SB_LOGIT_CAP)
            sp = jnp.log2(1.0 + jnp.exp2(zg))
            if mask_off is not None:
                strict = _causal_mask(bq, SB_GROUP, mask_off + g * SB_GROUP, strict=True)
                sp = jnp.where(strict, sp, 0.0)
            hi = sp.astype(BF16)
            lo = (sp - hi.astype(F32)).astype(BF16)
            between = jnp.dot(jnp.concatenate([hi, lo], axis=1), u2, preferred_element_type=F32)
            a = jnp.exp2((zg - sp) + between + _lanes(r, SB_GROUP))
            if mask_off is not None:
                a = jnp.where(strict, a, 0.0)
            parts[g] = a.astype(BF16)
            r = r - jnp.sum(sp, axis=-1, keepdims=True)
        r_ref[...] = r
        acc_ref[...] += jnp.dot(jnp.concatenate(parts, axis=1), _rows(v_ref, j, bk),
                                preferred_element_type=F32)

    logits(za_ref, 2 * i + 1)
    logits(zb_ref, 2 * i)
    step(za_ref, 2 * i + 1, mask_off=bk)
    logits(za_ref, 2 * i - 1)
    step(zb_ref, 2 * i, mask_off=0)

    def pair(t, c):
        j = 2 * i - 1 - 2 * t
        logits(zb_ref, j - 1)
        step(za_ref, j)
        logits(za_ref, j - 2)
        step(zb_ref, j - 1)
        return c

    lax.fori_loop(0, i, pair, 0)

    o_ref[0] = _rms(acc_ref[...], g_ref[0]).astype(o_ref.dtype)


def _attn_call(kernel, o_buf, head_off, n_heads, q_arr, k_arr, v_arr, gain, extra_in, extra_specs, scratch,
               dq, q_col, k_col, v_col, name):
    nb, seq, _ = q_arr.shape
    bq, bk = _attn_blocks(seq)
    in_specs = [pl.BlockSpec((1, bq, dq), lambda b, h, i: (b, i, q_col(h))),
                pl.BlockSpec((1, seq, HEAD_DIM), lambda b, h, i: (b, 0, k_col(h))),
                pl.BlockSpec((1, seq, HEAD_DIM), lambda b, h, i: (b, 0, v_col(h))),
                pl.BlockSpec((1, 1, HEAD_DIM), lambda b, h, i: (h, 0, 0))]
    in_specs += extra_specs + [pl.BlockSpec(memory_space=pl.ANY)]
    n_in = len(in_specs)
    return pl.pallas_call(
        functools.partial(kernel, bq=bq, bk=bk),
        out_shape=jax.ShapeDtypeStruct(o_buf.shape, o_buf.dtype),
        grid=(nb, n_heads, seq // bq),
        in_specs=in_specs,
        out_specs=pl.BlockSpec((1, bq, HEAD_DIM), lambda b, h, i: (b, i, head_off + h)),
        scratch_shapes=scratch,
        input_output_aliases={n_in - 1: 0},
        compiler_params=_params("arbitrary", "arbitrary", "arbitrary", flags=ATTN_FLAGS), name=name,
    )(q_arr, k_arr, v_arr, gain.reshape(n_heads, 1, HEAD_DIM).astype(F32), *extra_in, o_buf)


def _attn_blocks(seq):
    bk = min(ATTN_KEY_BLOCK, seq // 2)
    assert seq % (2 * bk) == 0 and bk % SB_GROUP == 0
    return 2 * bk, bk


def _stat_scratch(bq, n):
    out = []
    for _ in range(n):
        out += [pltpu.VMEM((bq, HEAD_DIM), F32), pltpu.VMEM((bq, 2 * HEAD_DIM), F32)]
    return out


def _logit_scratch(bq, bk, n):
    return [pltpu.VMEM((bq, bk), F32)] * n


def _group(off):
    return lambda h: off + h


def sb_attention(o_buf, head_off, qkv, gain, q_off, k_off, v_off):
    bq, bk = _attn_blocks(qkv.shape[1])
    u = -jnp.tril(jnp.ones((SB_GROUP, SB_GROUP), F32), -1).astype(BF16)
    u2 = jnp.concatenate([u, u], axis=0)
    return _attn_call(_sb_attn_kernel, o_buf, head_off, N_SB, qkv, qkv, qkv, gain, [u2],
                      [pl.BlockSpec((2 * SB_GROUP, SB_GROUP), lambda b, h, i: (0, 0))],
                      [pltpu.VMEM((bq, HEAD_DIM), F32), pltpu.VMEM((bq, HEAD_DIM), F32)]
                      + _logit_scratch(bq, bk, 2),
                      HEAD_DIM, _group(q_off), _group(k_off), _group(v_off), "sb_attention")


def _toeplitz(g, n):
    nh = g.shape[0]
    t = jnp.broadcast_to(g[:, None, :], (nh, n, 2 * n - 1))
    t = jnp.pad(t, ((0, 0), (0, 0), (0, 1))).reshape(nh, 2 * n * n)[:, :n * (2 * n - 1)]
    return t.reshape(nh, n, 2 * n - 1)[:, :, n - 1:]


def diff_attention(o_buf, head_off, qkv, gain, t5_table, lam, out_scale, q_off, k_off, v_off):
    seq = qkv.shape[1]
    bq, bk = _attn_blocks(seq)
    assert bk >= MAX_DISTANCE
    table = t5_table.astype(F32) * LOG2E
    f_table = table[_t5_bucket(jnp.arange(2 * bk, dtype=jnp.int32))].T
    x = jnp.arange(2 * bk - 1)
    b_diag = _toeplitz(f_table[:, jnp.maximum(bk - 1 - x, 0)], bk)
    b_sub = _toeplitz(f_table[:, 2 * bk - 1 - x], bk)
    far = jnp.broadcast_to(table[N_BUCKETS - 1][:, None, None], (N_DIFF, 1, HEAD_DIM))
    lam_row = jnp.broadcast_to(lam.astype(F32).reshape(1, 1), (1, HEAD_DIM))
    blk = pl.BlockSpec((1, bk, bk), lambda b, h, i: (h, 0, 0))
    return _attn_call(functools.partial(_diff_attn_kernel, out_scale=out_scale), o_buf, head_off, N_DIFF,
                      qkv, qkv, qkv, gain, [b_diag, b_sub, far, lam_row],
                      [blk, blk, pl.BlockSpec((1, 1, HEAD_DIM), lambda b, h, i: (h, 0, 0)),
                       pl.BlockSpec((1, HEAD_DIM), lambda b, h, i: (0, 0))],
                      [pltpu.VMEM((seq, 2 * HEAD_DIM), BF16)] + _stat_scratch(bq, 2)
                      + _logit_scratch(bq, bk, 4),
                      HEAD_DIM, _group(q_off), _group(k_off), _group(v_off), "diff_attention")


def _wide_softmax_attention(o_buf, head_off, fox, n_heads, q_arr, dq, q_col, kv_arr, k_col, v_col, kx, kx_spec,
                            gain, name):
    seq = q_arr.shape[1]
    bq, bk = _attn_blocks(seq)
    scratch = [pltpu.VMEM((seq, 2 * HEAD_DIM), BF16), pltpu.VMEM((seq, 2 * HEAD_DIM), BF16)]
    return _attn_call(functools.partial(_softmax_attn_kernel, fox=fox), o_buf, head_off, n_heads,
                      q_arr, kv_arr, kv_arr, gain, [kx], [kx_spec],
                      scratch + _stat_scratch(bq, 1) + _logit_scratch(bq, bk, 2),
                      dq, q_col, k_col, v_col, name)


def mla_attention(o_buf, head_off, q_cat, kv, k_rope_pad, gain):
    nb, seq, _ = q_cat.shape
    spec = pl.BlockSpec((1, 1, seq, HEAD_DIM), lambda b, h, i: (b, 0, 0, 0))
    return _wide_softmax_attention(o_buf, head_off, False, N_MLA, q_cat, MLA_QK_PAD, lambda h: h,
                                   kv, lambda h: 2 * h, lambda h: 2 * h + 1,
                                   k_rope_pad.reshape(nb, 1, seq, HEAD_DIM), spec, gain, "mla_attention")


def fox_attention(o_buf, head_off, qkv, gain, c, q_off, k_off, v_off):
    nb, seq, _ = qkv.shape
    c2 = jnp.moveaxis(c, -1, 1) * LOG2E
    hi = c2.astype(BF16)
    mid = (c2 - hi.astype(F32)).astype(BF16)
    lo = (c2 - hi.astype(F32) - mid.astype(F32)).astype(BF16)
    kx = jnp.stack([-hi, -mid, -lo], axis=-1)
    kx = jnp.pad(kx, ((0, 0), (0, 0), (0, 0), (0, HEAD_DIM - 3)))
    spec = pl.BlockSpec((1, 1, seq, HEAD_DIM), lambda b, h, i: (b, h, 0, 0))
    return _wide_softmax_attention(o_buf, head_off, True, N_FOX, qkv, HEAD_DIM, _group(q_off),
                                   qkv, _group(k_off), _group(v_off), kx, spec, gain, "fox_attention")


def _xattn_kernel(q_ref, k_ref, v_ref, o_ref):
    for h in range(N_XHEADS):
        sl = slice(h * HEAD_DIM, (h + 1) * HEAD_DIM)
        s = _qk(q_ref[0, :, sl], k_ref[0, :, sl])
        p = jnp.exp(s - jnp.max(s, axis=-1, keepdims=True))
        o = jnp.dot(p.astype(BF16), v_ref[0, :, sl], preferred_element_type=F32)
        o_ref[0, :, sl] = (o / jnp.sum(p, axis=-1, keepdims=True)).astype(o_ref.dtype)


def cross_attention(q, k, v, bq=512):
    nb, seq, w = q.shape
    n_mem = k.shape[1]
    bq = min(bq, seq)
    return pl.pallas_call(
        _xattn_kernel,
        out_shape=jax.ShapeDtypeStruct((nb, seq, w), BF16),
        grid=(nb, seq // bq),
        in_specs=[pl.BlockSpec((1, bq, w), lambda b, i: (b, i, 0)),
                  pl.BlockSpec((1, n_mem, w), lambda b, i: (b, 0, 0)),
                  pl.BlockSpec((1, n_mem, w), lambda b, i: (b, 0, 0))],
        out_specs=pl.BlockSpec((1, bq, w), lambda b, i: (b, i, 0)),
        compiler_params=_params("parallel", "parallel"), name="cross_attention",
    )(q, k, v)


def _t5_bucket(rel):
    n = jnp.maximum(rel, 0)
    max_exact = N_BUCKETS // 2
    nf = jnp.maximum(n, 1).astype(F32)
    large = max_exact + (jnp.log(nf / max_exact) / math.log(MAX_DISTANCE / max_exact)
                         * (N_BUCKETS - max_exact)).astype(jnp.int32)
    large = jnp.minimum(large, N_BUCKETS - 1)
    return jnp.where(n < max_exact, n, large)


def _rope(x, cos, sin):
    x1, x2 = jnp.split(x, 2, axis=-1)
    return jnp.concatenate([x1 * cos - x2 * sin, x2 * cos + x1 * sin], axis=-1)


SMALL_CKV = 2 * MLA_KV_RANK
SMALL_KR = SMALL_CKV + MLA_KV_RANK
SMALL_N = SMALL_KR + MLA_ROPE + N_FOX
SMALL_N_PAD = SMALL_N + (-SMALL_N) % 128


def _prep_w_in(w_in):
    nl, d, _ = w_in.shape
    hw = N_SB * HEAD_DIM
    o_df = 3 * hw
    o_cq = 6 * hw
    o_ckv = o_cq + MLA_Q_RANK
    o_fx = o_ckv + MLA_KV_RANK + MLA_ROPE
    o_fl = o_fx + 3 * hw
    sc = HEAD_DIM ** -0.5 * LOG2E
    sc_df = (HEAD_DIM // 2) ** -0.5 * LOG2E
    col_scale = jnp.ones((w_in.shape[2],), F32)
    col_scale = col_scale.at[0:hw].set(sc).at[o_df:o_df + hw].set(sc_df).at[o_fx:o_fx + hw].set(sc)
    w_bf = (w_in * col_scale).astype(BF16)
    w_main = jnp.concatenate([w_bf[..., :o_cq], w_bf[..., o_fx:o_fl]], axis=-1)
    w_small = jnp.concatenate([w_bf[..., o_cq:o_ckv], jnp.zeros((nl, d, SMALL_CKV - MLA_Q_RANK), BF16),
                               w_bf[..., o_ckv:o_fx], w_bf[..., o_fl:o_fl + N_FOX],
                               jnp.zeros((nl, d, SMALL_N_PAD - SMALL_N), BF16)], axis=-1)
    return w_main, w_small


def _mixer(h, nb, seq, cos, sin, t5_table, w_main, w_small, q_norm, w_uq, kv_norm, w_ukv, lam_vecs, b_forget,
           head_norm, w_out, layer_idx):
    hw = N_SB * HEAD_DIM
    sc_mla = (MLA_NOPE + MLA_ROPE) ** -0.5 * LOG2E
    qkv = matmul(h, w_main, BF16, layer=layer_idx).reshape(nb, seq, 9 * hw)
    small = matmul(h, w_small, F32, bm=512, bn=SMALL_N_PAD, layer=layer_idx)
    k_r = small[:, SMALL_KR:SMALL_KR + MLA_ROPE]
    f_logit = small[:, SMALL_KR + MLA_ROPE:SMALL_N]

    g_sb, g_df, g_mla, g_fx = jnp.split(head_norm, [N_SB, N_SB + N_DIFF, N_SB + N_DIFF + N_MLA], axis=0)
    o = jnp.zeros((nb, seq, (N_SB + N_DIFF + N_MLA + N_FOX) * HEAD_DIM), BF16)

    o = sb_attention(o, 0, qkv, g_sb, 0, N_SB, 2 * N_SB)

    lv = lam_vecs.astype(F32)
    lam_init = 0.8 - 0.6 * math.exp(-0.3 * layer_idx)
    lam = jnp.exp(jnp.sum(lv[0] * lv[1])) - jnp.exp(jnp.sum(lv[2] * lv[3])) + lam_init
    o = diff_attention(o, N_SB, qkv, g_df, t5_table, lam, 1.0 - lam_init, 3 * N_SB, 4 * N_SB, 5 * N_SB)

    wq = (w_uq * sc_mla).reshape(MLA_Q_RANK, N_MLA, MLA_NOPE + MLA_ROPE)
    w_r = wq[..., MLA_NOPE:]
    w_rot = jnp.concatenate([w_r[..., MLA_ROPE // 2:], w_r[..., :MLA_ROPE // 2]], axis=-1)
    wq = jnp.concatenate([wq, w_rot], axis=-1).reshape(MLA_Q_RANK, N_MLA * MLA_QK_PAD).astype(BF16)
    zeros = jnp.zeros_like(cos)
    cs = jnp.concatenate([cos, cos, zeros, zeros], axis=-1).reshape(nb * seq, HEAD_DIM)
    sn = jnp.concatenate([-sin, sin, zeros, zeros], axis=-1).reshape(nb * seq, HEAD_DIM)
    q_cat = norm_matmul(small, 0, MLA_Q_RANK, q_norm, wq, rope=(cs, sn, N_MLA))
    q_cat = q_cat.reshape(nb, seq, N_MLA * MLA_QK_PAD)
    kv = norm_matmul(small, SMALL_CKV // MLA_KV_RANK, MLA_KV_RANK, kv_norm, w_ukv.astype(BF16))
    kv = kv.reshape(nb, seq, N_MLA * 2 * HEAD_DIM)
    k_rope = _rope(k_r.reshape(nb, seq, MLA_ROPE), cos, sin)
    k_rope_pad = jnp.pad(k_rope, ((0, 0), (0, 0), (0, HEAD_DIM - MLA_ROPE))).astype(BF16)
    o = mla_attention(o, N_SB + N_DIFF, q_cat, kv, k_rope_pad, g_mla)

    log_f = jax.nn.log_sigmoid(f_logit.reshape(nb, seq, N_FOX) + b_forget.astype(F32))
    c = lax.cumsum(log_f, axis=1)
    o = fox_attention(o, N_SB + N_DIFF + N_MLA, qkv, g_fx, c, 6 * N_SB, 7 * N_SB, 8 * N_SB)

    return matmul(o.reshape(nb * seq, -1), w_out, F32, layer=layer_idx)


def kernel(x, mem, positions, t5_table, mix_norm_pre, w_in, mla_q_norm, w_uq, mla_kv_norm, w_ukv, diff_lambda, b_forget, head_norm, w_out, mix_norm_post, xattn_norm_pre, mem_norm, w_xq, w_xk, w_xv, w_xo, xattn_norm_post, ffn_norm_pre, w_gate, w_up, conv_w, conv_b, w_down, ffn_norm_post):
    nb, seq, d = x.shape
    depth = w_in.shape[0]
    n_mem = mem.shape[1]
    d_ff = w_gate.shape[2]
    xw = N_XHEADS * HEAD_DIM
    inv = ROPE_THETA ** (-jnp.arange(0, MLA_ROPE, 2, dtype=F32) / MLA_ROPE)
    ang = positions.astype(F32)[..., None] * inv
    cos, sin = jnp.cos(ang), jnp.sin(ang)

    xf = x.reshape(nb * seq, d)
    memf = mem.reshape(nb * n_mem, d)
    w_main, w_small = _prep_w_in(w_in)
    w_out, w_xo = w_out.astype(BF16), w_xo.astype(BF16)
    w_gate, w_up, w_down = w_gate.astype(BF16), w_up.astype(BF16), w_down.astype(BF16)
    h = rmsnorm(xf, mix_norm_pre[0])
    for l in range(depth):
        y = _mixer(h, nb, seq, cos, sin, t5_table, w_main, w_small, mla_q_norm[l], w_uq[l], mla_kv_norm[l],
                   w_ukv[l], diff_lambda[l], b_forget[l], head_norm[l], w_out, l)
        xf, h = addnorm(y, xf, mix_norm_post[l], xattn_norm_pre[l])

        mem_n = rmsnorm(memf, mem_norm[l])
        q = matmul(h, (w_xq[l] * HEAD_DIM ** -0.5).astype(BF16), BF16).reshape(nb, seq, xw)
        k = matmul(mem_n, w_xk[l].astype(BF16), BF16).reshape(nb, n_mem, xw)
        v = matmul(mem_n, w_xv[l].astype(BF16), BF16).reshape(nb, n_mem, xw)
        o = cross_attention(q, k, v).reshape(nb * seq, xw)
        xf, h = matmul_addnorm(o, w_xo, xf, xattn_norm_post[l], ffn_norm_pre[l], layer=l)

        act = ffn_up(h, w_gate, w_up, conv_w[l], conv_b[l], seq, layer=l)
        y = matmul(act, w_down, F32, bm=512, bn=512, layer=l)
        g_next = mix_norm_pre[l + 1] if l + 1 < depth else None
        xf, h = addnorm(y, xf, ffn_norm_post[l], g_next)
    return xf.reshape(nb, seq, d)
```

```python
import functools
import math

import jax
import jax.numpy as jnp
from jax import lax
from jax.experimental import pallas as pl
from jax.experimental.pallas import tpu as pltpu

F32 = jnp.float32
BF16 = jnp.bfloat16

HEAD_DIM = 128
N_SB = 8
N_DIFF = 8
N_MLA = 8
N_FOX = 8
MLA_Q_RANK = 768
MLA_KV_RANK = 512
MLA_NOPE = 128
MLA_ROPE = 64
MLA_QK_PAD = 256
N_XHEADS = 4
N_BUCKETS = 32
MAX_DISTANCE = 128
ROPE_THETA = 10000.0
EPS = 1e-6
NEG_BIG = -1e30
LOG2E = math.log2(math.e)

V7X_VMEM_LIMIT_BYTES = 56 * 1024 * 1024
ATTN_KEY_BLOCK = 512
SB_GROUP = 256
SB_LOGIT_CAP = 120.0
FFN_CHUNK = 256


def _params(*sem, flags=None):
    return pltpu.CompilerParams(dimension_semantics=sem, vmem_limit_bytes=V7X_VMEM_LIMIT_BYTES, flags=flags)


ATTN_FLAGS = None


def _rms(x, g):
    return x * lax.rsqrt(jnp.mean(x * x, axis=-1, keepdims=True) + EPS) * g


def _rmsnorm_kernel(x_ref, g_ref, o_ref):
    o_ref[...] = _rms(x_ref[...].astype(F32), g_ref[...]).astype(o_ref.dtype)


def rmsnorm(x, g, out_dtype=BF16, bm=512):
    m, d = x.shape
    bm = min(bm, m)
    return pl.pallas_call(
        _rmsnorm_kernel,
        out_shape=jax.ShapeDtypeStruct((m, d), out_dtype),
        grid=(pl.cdiv(m, bm),),
        in_specs=[pl.BlockSpec((bm, d), lambda i: (i, 0)), pl.BlockSpec((1, d), lambda i: (0, 0))],
        out_specs=pl.BlockSpec((bm, d), lambda i: (i, 0)),
        compiler_params=_params("parallel"),
        name="rmsnorm",
    )(x, g.reshape(1, d).astype(F32))


def _addnorm_kernel(y_ref, x_ref, gpost_ref, gnext_ref, xo_ref, ho_ref):
    xn = x_ref[...] + _rms(y_ref[...], gpost_ref[...])
    xo_ref[...] = xn
    ho_ref[...] = _rms(xn, gnext_ref[...]).astype(ho_ref.dtype)


def _addnorm_last_kernel(y_ref, x_ref, gpost_ref, xo_ref):
    xo_ref[...] = x_ref[...] + _rms(y_ref[...], gpost_ref[...])


def addnorm(y, x, g_post, g_next=None, bm=256):
    m, d = x.shape
    bm = min(bm, m)
    row = pl.BlockSpec((bm, d), lambda i: (i, 0))
    vec = pl.BlockSpec((1, d), lambda i: (0, 0))
    if g_next is None:
        return pl.pallas_call(
            _addnorm_last_kernel,
            out_shape=jax.ShapeDtypeStruct((m, d), F32),
            grid=(pl.cdiv(m, bm),), in_specs=[row, row, vec], out_specs=row,
            compiler_params=_params("parallel"), name="addnorm_last",
        )(y, x, g_post.reshape(1, d)), None
    return pl.pallas_call(
        _addnorm_kernel,
        out_shape=(jax.ShapeDtypeStruct((m, d), F32), jax.ShapeDtypeStruct((m, d), BF16)),
        grid=(pl.cdiv(m, bm),), in_specs=[row, row, vec, vec], out_specs=(row, row),
        compiler_params=_params("parallel"), name="addnorm",
    )(y, x, g_post.reshape(1, d), g_next.reshape(1, d))


def _matmul_kernel(a_ref, b_ref, o_ref):
    o_ref[...] = jnp.dot(a_ref[...], b_ref[...], preferred_element_type=F32).astype(o_ref.dtype)


def _matmul_acc_kernel(a_ref, b_ref, o_ref, acc_ref, *, nk):
    k = pl.program_id(2)

    @pl.when(k == 0)
    def _():
        acc_ref[...] = jnp.zeros_like(acc_ref)

    acc_ref[...] += jnp.dot(a_ref[...], b_ref[...], preferred_element_type=F32)

    @pl.when(k == nk - 1)
    def _():
        o_ref[...] = acc_ref[...].astype(o_ref.dtype)


def _pick_block(n, target):
    if n <= target:
        return n
    best = 128
    for cand in range(128, target + 1, 128):
        if n % cand == 0:
            best = cand
    assert n % best == 0
    return best


def _wspec(w, layer, block, index_map):
    if w.ndim == 2:
        return pl.BlockSpec(block, index_map)
    return pl.BlockSpec((None,) + tuple(block), lambda *g: (layer,) + tuple(index_map(*g)))


def matmul(a, b, out_dtype, bm=1024, bn=1024, bk=None, layer=None):
    m, kdim = a.shape
    n = b.shape[-1]
    bm, bn = _pick_block(m, bm), _pick_block(n, bn)
    bk = kdim if bk is None else bk
    assert kdim % bk == 0
    nk = kdim // bk
    if nk == 1:
        return pl.pallas_call(
            _matmul_kernel,
            out_shape=jax.ShapeDtypeStruct((m, n), out_dtype),
            grid=(pl.cdiv(m, bm), pl.cdiv(n, bn)),
            in_specs=[pl.BlockSpec((bm, kdim), lambda i, j: (i, 0)),
                      _wspec(b, layer, (kdim, bn), lambda i, j: (0, j))],
            out_specs=pl.BlockSpec((bm, bn), lambda i, j: (i, j)),
            compiler_params=_params("parallel", "parallel"), name="matmul",
        )(a, b)
    return pl.pallas_call(
        functools.partial(_matmul_acc_kernel, nk=nk),
        out_shape=jax.ShapeDtypeStruct((m, n), out_dtype),
        grid=(pl.cdiv(m, bm), pl.cdiv(n, bn), nk),
        in_specs=[pl.BlockSpec((bm, bk), lambda i, j, k: (i, k)),
                  _wspec(b, layer, (bk, bn), lambda i, j, k: (k, j))],
        out_specs=pl.BlockSpec((bm, bn), lambda i, j, k: (i, j)),
        scratch_shapes=[pltpu.VMEM((bm, bn), F32)],
        compiler_params=_params("parallel", "parallel", "arbitrary"), name="matmul_acc",
    )(a, b)


def _matmul_addnorm_kernel(a_ref, w_ref, x_ref, gpost_ref, gnext_ref, xo_ref, ho_ref):
    y = jnp.dot(a_ref[...], w_ref[...], preferred_element_type=F32)
    xn = x_ref[...] + _rms(y, gpost_ref[...])
    xo_ref[...] = xn
    ho_ref[...] = _rms(xn, gnext_ref[...]).astype(ho_ref.dtype)


def matmul_addnorm(a, w, x, g_post, g_next, bm=256, layer=None):
    m, kdim = a.shape
    d = w.shape[-1]
    bm = _pick_block(m, bm)
    row = pl.BlockSpec((bm, d), lambda i: (i, 0))
    vec = pl.BlockSpec((1, d), lambda i: (0, 0))
    return pl.pallas_call(
        _matmul_addnorm_kernel,
        out_shape=(jax.ShapeDtypeStruct((m, d), F32), jax.ShapeDtypeStruct((m, d), BF16)),
        grid=(m // bm,),
        in_specs=[pl.BlockSpec((bm, kdim), lambda i: (i, 0)), _wspec(w, layer, (kdim, d), lambda i: (0, 0)),
                  row, vec, vec],
        out_specs=(row, row),
        compiler_params=_params("parallel"), name="matmul_addnorm",
    )(a, w, x, g_post.reshape(1, d), g_next.reshape(1, d))


def _norm_matmul_kernel(x_ref, g_ref, w_ref, *rest, rope_heads):
    o_ref = rest[-1]
    xn = _rms(x_ref[...], g_ref[...]).astype(BF16)
    y = jnp.dot(xn, w_ref[...], preferred_element_type=F32)
    if not rope_heads:
        o_ref[...] = y.astype(o_ref.dtype)
        return
    cs, sn = rest[0][...], rest[1][...]
    for hd in range(rope_heads):
        c0 = hd * MLA_QK_PAD
        x = y[:, c0 + MLA_NOPE:c0 + MLA_QK_PAD]
        o_ref[:, c0:c0 + MLA_NOPE] = y[:, c0:c0 + MLA_NOPE].astype(o_ref.dtype)
        o_ref[:, c0 + MLA_NOPE:c0 + MLA_QK_PAD] = (x * cs + pltpu.roll(x, MLA_ROPE, axis=1) * sn).astype(o_ref.dtype)


def norm_matmul(x_arr, col_block, kdim, g, w, rope=None, bm=512, layer=None):
    m = x_arr.shape[0]
    n = w.shape[-1]
    bm = _pick_block(m, bm)
    in_specs = [pl.BlockSpec((bm, kdim), lambda i: (i, col_block)),
                pl.BlockSpec((1, kdim), lambda i: (0, 0)),
                _wspec(w, layer, (kdim, n), lambda i: (0, 0))]
    args = [x_arr, g.reshape(1, kdim).astype(F32), w]
    if rope is not None:
        in_specs += [pl.BlockSpec((bm, HEAD_DIM), lambda i: (i, 0))] * 2
        args += [rope[0], rope[1]]
    return pl.pallas_call(
        functools.partial(_norm_matmul_kernel, rope_heads=0 if rope is None else rope[2]),
        out_shape=jax.ShapeDtypeStruct((m, n), BF16),
        grid=(m // bm,), in_specs=in_specs,
        out_specs=pl.BlockSpec((bm, n), lambda i: (i, 0)),
        compiler_params=_params("parallel"), name="norm_matmul",
    )(*args)


def _ffn_up_kernel(h_ref, wg_ref, wu_ref, cw_ref, cb_ref, *rest, tiles_per_seq):
    o_ref, carry_ref = rest[-2:]
    i = pl.program_id(1)
    bm = h_ref.shape[0]
    bn = o_ref.shape[1]

    @pl.when(i % tiles_per_seq == 0)
    def _():
        carry_ref[...] = jnp.zeros_like(carry_ref)

    row8 = lax.broadcasted_iota(jnp.int32, (8, 1), 0)
    for c0 in range(0, bn, FFN_CHUNK):
        cols = slice(c0, c0 + FFN_CHUNK)
        gate = jnp.dot(h_ref[...], wg_ref[:, cols], preferred_element_type=F32)
        up = jnp.dot(h_ref[...], wu_ref[:, cols], preferred_element_type=F32)
        prev = carry_ref[:, cols]
        carry_ref[:, cols] = gate[bm - 8:, :]
        g1 = pltpu.roll(gate, 1, axis=0)
        g2 = pltpu.roll(gate, 2, axis=0)
        p1 = pltpu.roll(prev, 1, axis=0)
        p2 = pltpu.roll(prev, 2, axis=0)
        g1 = jnp.concatenate([jnp.where(row8 < 1, p1, g1[:8]), g1[8:]], axis=0)
        g2 = jnp.concatenate([jnp.where(row8 < 2, p2, g2[:8]), g2[8:]], axis=0)
        cw = cw_ref[:, cols]
        conv = cw[0:1] * g2 + cw[1:2] * g1 + cw[2:3] * gate + cb_ref[:, cols]
        o_ref[:, cols] = (jax.nn.gelu(conv, approximate=True) * up).astype(o_ref.dtype)


def _ffn_up_call(h, wg, wu, conv_w, conv_b, seq, out, col0, n_cols, bm, bn, layer):
    m, d = h.shape
    f = wg.shape[-1]
    assert n_cols % bn == 0 and col0 % bn == 0 and bn % FFN_CHUNK == 0
    jb = col0 // bn
    in_specs = [pl.BlockSpec((bm, d), lambda j, i: (i, 0)),
                _wspec(wg, layer, (d, bn), lambda j, i: (0, jb + j)),
                _wspec(wu, layer, (d, bn), lambda j, i: (0, jb + j)),
                pl.BlockSpec((3, bn), lambda j, i: (0, jb + j)),
                pl.BlockSpec((1, bn), lambda j, i: (0, jb + j))]
    args = [h, wg, wu, conv_w, conv_b]
    aliases = {}
    if out is not None:
        in_specs.append(pl.BlockSpec(memory_space=pl.ANY))
        args.append(out)
        aliases = {5: 0}
    return pl.pallas_call(
        functools.partial(_ffn_up_kernel, tiles_per_seq=seq // bm),
        out_shape=jax.ShapeDtypeStruct((m, f), BF16),
        grid=(n_cols // bn, m // bm),
        in_specs=in_specs,
        out_specs=pl.BlockSpec((bm, bn), lambda j, i: (i, jb + j)),
        scratch_shapes=[pltpu.VMEM((8, bn), F32)],
        input_output_aliases=aliases,
        compiler_params=_params("arbitrary", "arbitrary"), name="ffn_up",
    )(*args)


def ffn_up(h, wg, wu, conv_w, conv_b, seq, bm=512, bn=1024, layer=None):
    m, d = h.shape
    f = wg.shape[-1]
    bm = _pick_block(seq, bm)
    assert seq % bm == 0 and m % seq == 0 and bm % 8 == 0 and f % FFN_CHUNK == 0
    conv_w = conv_w.astype(F32)
    conv_b = conv_b.reshape(1, f).astype(F32)
    bn = min(bn, f)
    main = (f // bn) * bn
    out = _ffn_up_call(h, wg, wu, conv_w, conv_b, seq, None, 0, main, bm, bn, layer)
    if main < f:
        bn_tail = math.gcd(main, f - main)
        out = _ffn_up_call(h, wg, wu, conv_w, conv_b, seq, out, main, f - main, bm, bn_tail, layer)
    return out


def _qk(q, k):
    return lax.dot_general(q, k, (((1,), (1,)), ((), ())), preferred_element_type=F32)


def _lanes(x, n):
    return x if n == HEAD_DIM else jnp.concatenate([x] * (n // HEAD_DIM), axis=1)


def _rows(ref, j, bk):
    return ref[pl.ds(pl.multiple_of(j * bk, bk), bk), :]


def _causal_mask(bq, bk, col_off, strict=False):
    row = lax.broadcasted_iota(jnp.int32, (bq, bk), 0)
    col = lax.broadcasted_iota(jnp.int32, (bq, bk), 1) + col_off
    return col < row if strict else col <= row


def _softmax_update(s_ref, v_aug, m_ref, acc_ref, bias=None, mask_off=None, row_start=0):
    bq, bk = s_ref.shape
    live = slice(row_start, bq)
    if bias is None:
        s = s_ref[live, :]
    else:
        s = jnp.concatenate([s_ref[r0:r0 + bk, :] + bias[r0 // bk] for r0 in range(row_start, bq, bk)], axis=0)
    if mask_off is not None:
        row = lax.broadcasted_iota(jnp.int32, s.shape, 0) + row_start
        col = lax.broadcasted_iota(jnp.int32, s.shape, 1) + mask_off
        s = jnp.where(col <= row, s, NEG_BIG)
    m_prev = m_ref[live, :]
    m_new = jnp.maximum(m_prev, jnp.max(s, axis=-1, keepdims=True))
    alpha = jnp.exp2(m_prev - m_new)
    p = jnp.exp2(s - _lanes(m_new, bk))
    pv = jnp.dot(p.astype(BF16), v_aug, preferred_element_type=F32)
    acc_ref[live, :] = _lanes(alpha, 2 * HEAD_DIM) * acc_ref[live, :] + pv
    m_ref[live, :] = m_new


def _init_stats(m_ref, acc_ref):
    m_ref[...] = jnp.full_like(m_ref, NEG_BIG)
    acc_ref[...] = jnp.zeros_like(acc_ref)


def _normalised(acc_ref):
    acc = acc_ref[...]
    return acc[:, :HEAD_DIM] / acc[:, HEAD_DIM:]


def _fill_v_aug(vaug_ref, v_ref):
    vaug_ref[:, :HEAD_DIM] = v_ref[0]
    vaug_ref[:, HEAD_DIM:] = jnp.ones((vaug_ref.shape[0], HEAD_DIM), BF16)


def _softmax_attn_kernel(q_ref, k_ref, v_ref, g_ref, kx_ref, _, o_ref, kaug_ref, vaug_ref, m_ref, acc_ref,
                         sa_ref, sb_ref, *, fox, bq, bk):
    i = pl.program_id(2)

    @pl.when(i == 0)
    def _():
        kaug_ref[:, :HEAD_DIM] = k_ref[0]
        kaug_ref[:, HEAD_DIM:] = kx_ref[0, 0]
        _fill_v_aug(vaug_ref, v_ref)

    q = q_ref[0]
    if fox:
        lane = lax.broadcasted_iota(jnp.int32, (bq, HEAD_DIM), 1)
        q = jnp.concatenate([q, jnp.where(lane < 3, 1.0, 0.0).astype(BF16)], axis=1)
    _init_stats(m_ref, acc_ref)

    def logits(s_ref, j, row_start=0):
        s_ref[row_start:, :] = _qk(q[row_start:], _rows(kaug_ref, j, bk))

    def update(s_ref, j, **kw):
        _softmax_update(s_ref, _rows(vaug_ref, j, bk), m_ref, acc_ref, **kw)

    logits(sa_ref, 0)

    def pair(t, c):
        j = 2 * t
        logits(sb_ref, j + 1)
        update(sa_ref, j)
        logits(sa_ref, j + 2)
        update(sb_ref, j + 1)
        return c

    lax.fori_loop(0, i, pair, 0)
    logits(sb_ref, 2 * i + 1, row_start=bk)
    update(sa_ref, 2 * i, mask_off=0)
    update(sb_ref, 2 * i + 1, mask_off=bk, row_start=bk)
    o_ref[0] = _rms(_normalised(acc_ref), g_ref[0]).astype(o_ref.dtype)


def _diff_attn_kernel(q_ref, k_ref, v_ref, g_ref, bdiag_ref, bsub_ref, far_ref, lam_ref, _, o_ref,
                      vaug_ref, m1_ref, a1_ref, m2_ref, a2_ref, s1a_ref, s2a_ref, s1b_ref, s2b_ref,
                      *, bq, bk, out_scale):
    i = pl.program_id(2)

    @pl.when(i == 0)
    def _():
        _fill_v_aug(vaug_ref, v_ref)

    q = q_ref[0]
    half = lax.broadcasted_iota(jnp.int32, q.shape, 1) < (HEAD_DIM // 2)
    zero = jnp.zeros_like(q)
    q1 = jnp.where(half, q, zero)
    q2 = jnp.where(half, zero, q)
    _init_stats(m1_ref, a1_ref)
    _init_stats(m2_ref, a2_ref)
    k_ref = k_ref.at[0]
    buf_a = (s1a_ref, s2a_ref)
    buf_b = (s1b_ref, s2b_ref)

    def logits(buf, j, row_start=0):
        k = _rows(k_ref, j, bk)
        buf[0][row_start:, :] = _qk(q1[row_start:], k)
        buf[1][row_start:, :] = _qk(q2[row_start:], k)

    far = _lanes(far_ref[0], bk)

    def update(buf, j, bias_top, bias_bottom, **kw):
        v = _rows(vaug_ref, j, bk)
        bias = None if bias_top is None and bias_bottom is None else (bias_top, bias_bottom)
        for s_ref, m_ref, a_ref in ((buf[0], m1_ref, a1_ref), (buf[1], m2_ref, a2_ref)):
            _softmax_update(s_ref, v, m_ref, a_ref, bias=bias, **kw)

    logits(buf_a, 0)

    def pair(t, c):
        j = 2 * t
        logits(buf_b, j + 1)
        update(buf_a, j, None, None)
        logits(buf_a, j + 2)
        update(buf_b, j + 1, None, None)
        return c

    lax.fori_loop(0, i - 1, pair, 0)
    m1_ref[...] = m1_ref[...] + far_ref[0]
    m2_ref[...] = m2_ref[...] + far_ref[0]

    @pl.when(i >= 1)
    def _():
        logits(buf_b, 2 * i - 1)
        update(buf_a, 2 * i - 2, far, far)
        logits(buf_a, 2 * i)
        update(buf_b, 2 * i - 1, bsub_ref[0], far)

    logits(buf_b, 2 * i + 1, row_start=bk)
    update(buf_a, 2 * i, bdiag_ref[0], bsub_ref[0], mask_off=0)
    update(buf_b, 2 * i + 1, None, bdiag_ref[0], mask_off=bk, row_start=bk)
    o = _normalised(a1_ref) - lam_ref[...] * _normalised(a2_ref)
    o_ref[0] = (_rms(o, g_ref[0]) * out_scale).astype(o_ref.dtype)


def _sb_attn_kernel(q_ref, k_ref, v_ref, g_ref, u_ref, _, o_ref, r_ref, acc_ref, za_ref, zb_ref, *, bq, bk):
    i = pl.program_id(2)
    q = q_ref[0]
    k_ref = k_ref.at[0]
    v_ref = v_ref.at[0]
    r_ref[...] = jnp.zeros_like(r_ref)
    acc_ref[...] = jnp.zeros_like(acc_ref)
    u = u_ref[...]
    n_groups = bk // SB_GROUP

    def logits(z_ref, j):
        z_ref[...] = _qk(q, _rows(k_ref, jnp.maximum(j, 0), bk))

    def step(z_ref, j, mask_off=None):
        z = z_ref[...]
        r = r_ref[...]
        parts = [None] * n_groups
        for g in reversed(range(n_groups)):
            zg = jnp.minimum(z[:, g * SB_GROUP:(g + 1) * SB_GROUP], SB_LOGIT_CAP)
            sp = jnp.log2(1.0 + jnp.exp2(zg))
            if mask_off is not None:
                strict = _causal_mask(bq, SB_GROUP, mask_off + g * SB_GROUP, strict=True)
                sp = jnp.where(strict, sp, 0.0)
            between = jnp.dot(sp.astype(BF16), u, preferred_element_type=F32)
            a = jnp.exp2((zg - sp) + between + _lanes(r, SB_GROUP))
            if mask_off is not None:
                a = jnp.where(strict, a, 0.0)
            parts[g] = a.astype(BF16)
            r = r - jnp.sum(sp, axis=-1, keepdims=True)
        r_ref[...] = r
        acc_ref[...] += jnp.dot(jnp.concatenate(parts, axis=1), _rows(v_ref, j, bk),
                                preferred_element_type=F32)

    logits(za_ref, 2 * i + 1)
    logits(zb_ref, 2 * i)
    step(za_ref, 2 * i + 1, mask_off=bk)
    logits(za_ref, 2 * i - 1)
    step(zb_ref, 2 * i, mask_off=0)

    def pair(t, c):
        j = 2 * i - 1 - 2 * t
        logits(zb_ref, j - 1)
        step(za_ref, j)
        logits(za_ref, j - 2)
        step(zb_ref, j - 1)
        return c

    lax.fori_loop(0, i, pair, 0)

    o_ref[0] = _rms(acc_ref[...], g_ref[0]).astype(o_ref.dtype)


def _attn_call(kernel, o_buf, head_off, n_heads, q_arr, k_arr, v_arr, gain, extra_in, extra_specs, scratch,
               dq, q_col, k_col, v_col, name):
    nb, seq, _ = q_arr.shape
    bq, bk = _attn_blocks(seq)
    in_specs = [pl.BlockSpec((1, bq, dq), lambda b, h, i: (b, i, q_col(h))),
                pl.BlockSpec((1, seq, HEAD_DIM), lambda b, h, i: (b, 0, k_col(h))),
                pl.BlockSpec((1, seq, HEAD_DIM), lambda b, h, i: (b, 0, v_col(h))),
                pl.BlockSpec((1, 1, HEAD_DIM), lambda b, h, i: (h, 0, 0))]
    in_specs += extra_specs + [pl.BlockSpec(memory_space=pl.ANY)]
    n_in = len(in_specs)
    return pl.pallas_call(
        functools.partial(kernel, bq=bq, bk=bk),
        out_shape=jax.ShapeDtypeStruct(o_buf.shape, o_buf.dtype),
        grid=(nb, n_heads, seq // bq),
        in_specs=in_specs,
        out_specs=pl.BlockSpec((1, bq, HEAD_DIM), lambda b, h, i: (b, i, head_off + h)),
        scratch_shapes=scratch,
        input_output_aliases={n_in - 1: 0},
        compiler_params=_params("arbitrary", "arbitrary", "arbitrary", flags=ATTN_FLAGS), name=name,
    )(q_arr, k_arr, v_arr, gain.reshape(n_heads, 1, HEAD_DIM).astype(F32), *extra_in, o_buf)


def _attn_blocks(seq):
    bk = min(ATTN_KEY_BLOCK, seq // 2)
    assert seq % (2 * bk) == 0 and bk % SB_GROUP == 0
    return 2 * bk, bk


def _stat_scratch(bq, n):
    out = []
    for _ in range(n):
        out += [pltpu.VMEM((bq, HEAD_DIM), F32), pltpu.VMEM((bq, 2 * HEAD_DIM), F32)]
    return out


def _logit_scratch(bq, bk, n):
    return [pltpu.VMEM((bq, bk), F32)] * n


def _group(off):
    return lambda h: off + h


def sb_attention(o_buf, head_off, qkv, gain, q_off, k_off, v_off):
    bq, bk = _attn_blocks(qkv.shape[1])
    u = -jnp.tril(jnp.ones((SB_GROUP, SB_GROUP), F32), -1).astype(BF16)
    return _attn_call(_sb_attn_kernel, o_buf, head_off, N_SB, qkv, qkv, qkv, gain, [u],
                      [pl.BlockSpec((SB_GROUP, SB_GROUP), lambda b, h, i: (0, 0))],
                      [pltpu.VMEM((bq, HEAD_DIM), F32), pltpu.VMEM((bq, HEAD_DIM), F32)]
                      + _logit_scratch(bq, bk, 2),
                      HEAD_DIM, _group(q_off), _group(k_off), _group(v_off), "sb_attention")


def _toeplitz(g, n):
    nh = g.shape[0]
    t = jnp.broadcast_to(g[:, None, :], (nh, n, 2 * n - 1))
    t = jnp.pad(t, ((0, 0), (0, 0), (0, 1))).reshape(nh, 2 * n * n)[:, :n * (2 * n - 1)]
    return t.reshape(nh, n, 2 * n - 1)[:, :, n - 1:]


def diff_attention(o_buf, head_off, qkv, gain, t5_table, lam, out_scale, q_off, k_off, v_off):
    seq = qkv.shape[1]
    bq, bk = _attn_blocks(seq)
    assert bk >= MAX_DISTANCE
    table = t5_table.astype(F32) * LOG2E
    f_table = table[_t5_bucket(jnp.arange(2 * bk, dtype=jnp.int32))].T
    x = jnp.arange(2 * bk - 1)
    b_diag = _toeplitz(f_table[:, jnp.maximum(bk - 1 - x, 0)], bk)
    b_sub = _toeplitz(f_table[:, 2 * bk - 1 - x], bk)
    far = jnp.broadcast_to(table[N_BUCKETS - 1][:, None, None], (N_DIFF, 1, HEAD_DIM))
    lam_row = jnp.broadcast_to(lam.astype(F32).reshape(1, 1), (1, HEAD_DIM))
    blk = pl.BlockSpec((1, bk, bk), lambda b, h, i: (h, 0, 0))
    return _attn_call(functools.partial(_diff_attn_kernel, out_scale=out_scale), o_buf, head_off, N_DIFF,
                      qkv, qkv, qkv, gain, [b_diag, b_sub, far, lam_row],
                      [blk, blk, pl.BlockSpec((1, 1, HEAD_DIM), lambda b, h, i: (h, 0, 0)),
                       pl.BlockSpec((1, HEAD_DIM), lambda b, h, i: (0, 0))],
                      [pltpu.VMEM((seq, 2 * HEAD_DIM), BF16)] + _stat_scratch(bq, 2)
                      + _logit_scratch(bq, bk, 4),
                      HEAD_DIM, _group(q_off), _group(k_off), _group(v_off), "diff_attention")


def _wide_softmax_attention(o_buf, head_off, fox, n_heads, q_arr, dq, q_col, kv_arr, k_col, v_col, kx, kx_spec,
                            gain, name):
    seq = q_arr.shape[1]
    bq, bk = _attn_blocks(seq)
    scratch = [pltpu.VMEM((seq, 2 * HEAD_DIM), BF16), pltpu.VMEM((seq, 2 * HEAD_DIM), BF16)]
    return _attn_call(functools.partial(_softmax_attn_kernel, fox=fox), o_buf, head_off, n_heads,
                      q_arr, kv_arr, kv_arr, gain, [kx], [kx_spec],
                      scratch + _stat_scratch(bq, 1) + _logit_scratch(bq, bk, 2),
                      dq, q_col, k_col, v_col, name)


def mla_attention(o_buf, head_off, q_cat, kv, k_rope_pad, gain):
    nb, seq, _ = q_cat.shape
    spec = pl.BlockSpec((1, 1, seq, HEAD_DIM), lambda b, h, i: (b, 0, 0, 0))
    return _wide_softmax_attention(o_buf, head_off, False, N_MLA, q_cat, MLA_QK_PAD, lambda h: h,
                                   kv, lambda h: 2 * h, lambda h: 2 * h + 1,
                                   k_rope_pad.reshape(nb, 1, seq, HEAD_DIM), spec, gain, "mla_attention")


def fox_attention(o_buf, head_off, qkv, gain, c, q_off, k_off, v_off):
    nb, seq, _ = qkv.shape
    c2 = jnp.moveaxis(c, -1, 1) * LOG2E
    hi = c2.astype(BF16)
    mid = (c2 - hi.astype(F32)).astype(BF16)
    lo = (c2 - hi.astype(F32) - mid.astype(F32)).astype(BF16)
    kx = jnp.stack([-hi, -mid, -lo], axis=-1)
    kx = jnp.pad(kx, ((0, 0), (0, 0), (0, 0), (0, HEAD_DIM - 3)))
    spec = pl.BlockSpec((1, 1, seq, HEAD_DIM), lambda b, h, i: (b, h, 0, 0))
    return _wide_softmax_attention(o_buf, head_off, True, N_FOX, qkv, HEAD_DIM, _group(q_off),
                                   qkv, _group(k_off), _group(v_off), kx, spec, gain, "fox_attention")


def _xattn_kernel(q_ref, k_ref, v_ref, o_ref):
    for h in range(N_XHEADS):
        sl = slice(h * HEAD_DIM, (h + 1) * HEAD_DIM)
        s = _qk(q_ref[0, :, sl], k_ref[0, :, sl])
        p = jnp.exp(s - jnp.max(s, axis=-1, keepdims=True))
        o = jnp.dot(p.astype(BF16), v_ref[0, :, sl], preferred_element_type=F32)
        o_ref[0, :, sl] = (o / jnp.sum(p, axis=-1, keepdims=True)).astype(o_ref.dtype)


def cross_attention(q, k, v, bq=512):
    nb, seq, w = q.shape
    n_mem = k.shape[1]
    bq = min(bq, seq)
    return pl.pallas_call(
        _xattn_kernel,
        out_shape=jax.ShapeDtypeStruct((nb, seq, w), BF16),
        grid=(nb, seq // bq),
        in_specs=[pl.BlockSpec((1, bq, w), lambda b, i: (b, i, 0)),
                  pl.BlockSpec((1, n_mem, w), lambda b, i: (b, 0, 0)),
                  pl.BlockSpec((1, n_mem, w), lambda b, i: (b, 0, 0))],
        out_specs=pl.BlockSpec((1, bq, w), lambda b, i: (b, i, 0)),
        compiler_params=_params("parallel", "parallel"), name="cross_attention",
    )(q, k, v)


def _t5_bucket(rel):
    n = jnp.maximum(rel, 0)
    max_exact = N_BUCKETS // 2
    nf = jnp.maximum(n, 1).astype(F32)
    large = max_exact + (jnp.log(nf / max_exact) / math.log(MAX_DISTANCE / max_exact)
                         * (N_BUCKETS - max_exact)).astype(jnp.int32)
    large = jnp.minimum(large, N_BUCKETS - 1)
    return jnp.where(n < max_exact, n, large)


def _rope(x, cos, sin):
    x1, x2 = jnp.split(x, 2, axis=-1)
    return jnp.concatenate([x1 * cos - x2 * sin, x2 * cos + x1 * sin], axis=-1)


HW = N_SB * HEAD_DIM
IN_DF = 3 * HW
IN_CQ = 6 * HW
IN_CKV = IN_CQ + MLA_Q_RANK
IN_KR = IN_CKV + MLA_KV_RANK
IN_FX = IN_KR + MLA_ROPE
IN_FL = IN_FX + 3 * HW
SMALL_CKV = 2 * MLA_KV_RANK
SMALL_KR = SMALL_CKV + MLA_KV_RANK
SMALL_FL = SMALL_KR + HEAD_DIM
SMALL_N = SMALL_FL + HEAD_DIM


def _prep_w_in_kernel(w_ref, scale_ref, main_ref, small_ref):
    w = w_ref[...] * scale_ref[...]
    main_ref[:, :IN_CQ] = w[:, :IN_CQ].astype(BF16)
    main_ref[:, IN_CQ:] = w[:, IN_FX:IN_FL].astype(BF16)
    small_ref[...] = jnp.zeros_like(small_ref)
    small_ref[:, :MLA_Q_RANK] = w[:, IN_CQ:IN_CKV].astype(BF16)
    small_ref[:, SMALL_CKV:SMALL_KR] = w[:, IN_CKV:IN_KR].astype(BF16)
    small_ref[:, SMALL_KR:SMALL_KR + MLA_ROPE] = w[:, IN_KR:IN_FX].astype(BF16)
    small_ref[:, SMALL_FL:SMALL_FL + N_FOX] = w[:, IN_FL:IN_FL + N_FOX].astype(BF16)


def _prep_w_in(w_in, bd=128):
    nl, d, n_in = w_in.shape
    sc = HEAD_DIM ** -0.5 * LOG2E
    sc_df = (HEAD_DIM // 2) ** -0.5 * LOG2E
    col_scale = jnp.ones((1, n_in), F32)
    col_scale = col_scale.at[:, 0:HW].set(sc).at[:, IN_DF:IN_DF + HW].set(sc_df).at[:, IN_FX:IN_FX + HW].set(sc)
    bd = _pick_block(d, bd)
    return pl.pallas_call(
        _prep_w_in_kernel,
        out_shape=(jax.ShapeDtypeStruct((nl, d, 9 * HW), BF16), jax.ShapeDtypeStruct((nl, d, SMALL_N), BF16)),
        grid=(nl, d // bd),
        in_specs=[pl.BlockSpec((None, bd, n_in), lambda l, i: (l, i, 0)),
                  pl.BlockSpec((1, n_in), lambda l, i: (0, 0))],
        out_specs=(pl.BlockSpec((None, bd, 9 * HW), lambda l, i: (l, i, 0)),
                   pl.BlockSpec((None, bd, SMALL_N), lambda l, i: (l, i, 0))),
        compiler_params=_params("parallel", "parallel"), name="prep_w_in",
    )(w_in, col_scale)


def _mixer(h, nb, seq, cos, sin, t5_table, w_main, w_small, q_norm, w_uq, kv_norm, w_ukv, lam_vecs, b_forget,
           head_norm, w_out, layer_idx):
    sc_mla = (MLA_NOPE + MLA_ROPE) ** -0.5 * LOG2E
    qkv = matmul(h, w_main, BF16, layer=layer_idx).reshape(nb, seq, 9 * HW)
    small = matmul(h, w_small, F32, bm=512, bn=SMALL_N, layer=layer_idx)
    k_r = small[:, SMALL_KR:SMALL_KR + MLA_ROPE]
    f_logit = small[:, SMALL_FL:SMALL_FL + N_FOX]

    g_sb, g_df, g_mla, g_fx = jnp.split(head_norm, [N_SB, N_SB + N_DIFF, N_SB + N_DIFF + N_MLA], axis=0)
    o = jnp.zeros((nb, seq, (N_SB + N_DIFF + N_MLA + N_FOX) * HEAD_DIM), BF16)

    o = sb_attention(o, 0, qkv, g_sb, 0, N_SB, 2 * N_SB)

    lv = lam_vecs.astype(F32)
    lam_init = 0.8 - 0.6 * math.exp(-0.3 * layer_idx)
    lam = jnp.exp(jnp.sum(lv[0] * lv[1])) - jnp.exp(jnp.sum(lv[2] * lv[3])) + lam_init
    o = diff_attention(o, N_SB, qkv, g_df, t5_table, lam, 1.0 - lam_init, 3 * N_SB, 4 * N_SB, 5 * N_SB)

    wq = (w_uq * sc_mla).reshape(MLA_Q_RANK, N_MLA, MLA_NOPE + MLA_ROPE)
    w_r = wq[..., MLA_NOPE:]
    w_rot = jnp.concatenate([w_r[..., MLA_ROPE // 2:], w_r[..., :MLA_ROPE // 2]], axis=-1)
    wq = jnp.concatenate([wq, w_rot], axis=-1).reshape(MLA_Q_RANK, N_MLA * MLA_QK_PAD).astype(BF16)
    zeros = jnp.zeros_like(cos)
    cs = jnp.concatenate([cos, cos, zeros, zeros], axis=-1).reshape(nb * seq, HEAD_DIM)
    sn = jnp.concatenate([-sin, sin, zeros, zeros], axis=-1).reshape(nb * seq, HEAD_DIM)
    q_cat = norm_matmul(small, 0, MLA_Q_RANK, q_norm, wq, rope=(cs, sn, N_MLA))
    q_cat = q_cat.reshape(nb, seq, N_MLA * MLA_QK_PAD)
    kv = norm_matmul(small, SMALL_CKV // MLA_KV_RANK, MLA_KV_RANK, kv_norm, w_ukv.astype(BF16))
    kv = kv.reshape(nb, seq, N_MLA * 2 * HEAD_DIM)
    k_rope = _rope(k_r.reshape(nb, seq, MLA_ROPE), cos, sin)
    k_rope_pad = jnp.pad(k_rope, ((0, 0), (0, 0), (0, HEAD_DIM - MLA_ROPE))).astype(BF16)
    o = mla_attention(o, N_SB + N_DIFF, q_cat, kv, k_rope_pad, g_mla)

    log_f = jax.nn.log_sigmoid(f_logit.reshape(nb, seq, N_FOX) + b_forget.astype(F32))
    c = lax.cumsum(log_f, axis=1)
    o = fox_attention(o, N_SB + N_DIFF + N_MLA, qkv, g_fx, c, 6 * N_SB, 7 * N_SB, 8 * N_SB)

    return matmul(o.reshape(nb * seq, -1), w_out, F32, layer=layer_idx)


def kernel(x, mem, positions, t5_table, mix_norm_pre, w_in, mla_q_norm, w_uq, mla_kv_norm, w_ukv, diff_lambda, b_forget, head_norm, w_out, mix_norm_post, xattn_norm_pre, mem_norm, w_xq, w_xk, w_xv, w_xo, xattn_norm_post, ffn_norm_pre, w_gate, w_up, conv_w, conv_b, w_down, ffn_norm_post):
    nb, seq, d = x.shape
    depth = w_in.shape[0]
    n_mem = mem.shape[1]
    d_ff = w_gate.shape[2]
    xw = N_XHEADS * HEAD_DIM
    inv = ROPE_THETA ** (-jnp.arange(0, MLA_ROPE, 2, dtype=F32) / MLA_ROPE)
    ang = positions.astype(F32)[..., None] * inv
    cos, sin = jnp.cos(ang), jnp.sin(ang)

    xf = x.reshape(nb * seq, d)
    memf = mem.reshape(nb * n_mem, d)
    w_main, w_small = _prep_w_in(w_in)
    w_out, w_xo = w_out.astype(BF16), w_xo.astype(BF16)
    w_gate, w_up, w_down = w_gate.astype(BF16), w_up.astype(BF16), w_down.astype(BF16)
    h = rmsnorm(xf, mix_norm_pre[0])
    for l in range(depth):
        y = _mixer(h, nb, seq, cos, sin, t5_table, w_main, w_small, mla_q_norm[l], w_uq[l], mla_kv_norm[l],
                   w_ukv[l], diff_lambda[l], b_forget[l], head_norm[l], w_out, l)
        xf, h = addnorm(y, xf, mix_norm_post[l], xattn_norm_pre[l])

        mem_n = rmsnorm(memf, mem_norm[l])
        q = matmul(h, (w_xq[l] * HEAD_DIM ** -0.5).astype(BF16), BF16).reshape(nb, seq, xw)
        k = matmul(mem_n, w_xk[l].astype(BF16), BF16).reshape(nb, n_mem, xw)
        v = matmul(mem_n, w_xv[l].astype(BF16), BF16).reshape(nb, n_mem, xw)
        o = cross_attention(q, k, v).reshape(nb * seq, xw)
        xf, h = matmul_addnorm(o, w_xo, xf, xattn_norm_post[l], ffn_norm_pre[l], layer=l)

        act = ffn_up(h, w_gate, w_up, conv_w[l], conv_b[l], seq, layer=l)
        y = matmul(act, w_down, F32, bm=512, bn=512, layer=l)
        g_next = mix_norm_pre[l + 1] if l + 1 < depth else None
        xf, h = addnorm(y, xf, ffn_norm_post[l], g_next)
    return xf.reshape(nb, seq, d)
```

```python
import functools
import math

import jax
import jax.numpy as jnp
from jax import lax
from jax.experimental import pallas as pl
from jax.experimental.pallas import tpu as pltpu

F32 = jnp.float32
BF16 = jnp.bfloat16

HEAD_DIM = 128
N_SB = 8
N_DIFF = 8
N_MLA = 8
N_FOX = 8
MLA_Q_RANK = 768
MLA_KV_RANK = 512
MLA_NOPE = 128
MLA_ROPE = 64
MLA_QK_PAD = 256
N_XHEADS = 4
N_BUCKETS = 32
MAX_DISTANCE = 128
ROPE_THETA = 10000.0
EPS = 1e-6
NEG_BIG = -1e30
LOG2E = math.log2(math.e)

V7X_VMEM_LIMIT_BYTES = 56 * 1024 * 1024
ATTN_KEY_BLOCK = 512
SB_GROUP = 256
SB_LOGIT_CAP = 120.0
FFN_CHUNK = 256


def _params(*sem, flags=None):
    return pltpu.CompilerParams(dimension_semantics=sem, vmem_limit_bytes=V7X_VMEM_LIMIT_BYTES, flags=flags)


ATTN_FLAGS = None


def _rms(x, g):
    return x * lax.rsqrt(jnp.mean(x * x, axis=-1, keepdims=True) + EPS) * g


def _rmsnorm_kernel(x_ref, g_ref, o_ref):
    o_ref[...] = _rms(x_ref[...].astype(F32), g_ref[...]).astype(o_ref.dtype)


def rmsnorm(x, g, out_dtype=BF16, bm=512):
    m, d = x.shape
    bm = min(bm, m)
    return pl.pallas_call(
        _rmsnorm_kernel,
        out_shape=jax.ShapeDtypeStruct((m, d), out_dtype),
        grid=(pl.cdiv(m, bm),),
        in_specs=[pl.BlockSpec((bm, d), lambda i: (i, 0)), pl.BlockSpec((1, d), lambda i: (0, 0))],
        out_specs=pl.BlockSpec((bm, d), lambda i: (i, 0)),
        compiler_params=_params("parallel"),
        name="rmsnorm",
    )(x, g.reshape(1, d).astype(F32))


def _addnorm_kernel(y_ref, x_ref, gpost_ref, gnext_ref, xo_ref, ho_ref):
    xn = x_ref[...] + _rms(y_ref[...], gpost_ref[...])
    xo_ref[...] = xn
    ho_ref[...] = _rms(xn, gnext_ref[...]).astype(ho_ref.dtype)


def _addnorm_last_kernel(y_ref, x_ref, gpost_ref, xo_ref):
    xo_ref[...] = x_ref[...] + _rms(y_ref[...], gpost_ref[...])


def addnorm(y, x, g_post, g_next=None, bm=256):
    m, d = x.shape
    bm = min(bm, m)
    row = pl.BlockSpec((bm, d), lambda i: (i, 0))
    vec = pl.BlockSpec((1, d), lambda i: (0, 0))
    if g_next is None:
        return pl.pallas_call(
            _addnorm_last_kernel,
            out_shape=jax.ShapeDtypeStruct((m, d), F32),
            grid=(pl.cdiv(m, bm),), in_specs=[row, row, vec], out_specs=row,
            compiler_params=_params("parallel"), name="addnorm_last",
        )(y, x, g_post.reshape(1, d)), None
    return pl.pallas_call(
        _addnorm_kernel,
        out_shape=(jax.ShapeDtypeStruct((m, d), F32), jax.ShapeDtypeStruct((m, d), BF16)),
        grid=(pl.cdiv(m, bm),), in_specs=[row, row, vec, vec], out_specs=(row, row),
        compiler_params=_params("parallel"), name="addnorm",
    )(y, x, g_post.reshape(1, d), g_next.reshape(1, d))


def _matmul_kernel(a_ref, b_ref, o_ref):
    o_ref[...] = jnp.dot(a_ref[...], b_ref[...], preferred_element_type=F32).astype(o_ref.dtype)


def _matmul_acc_kernel(a_ref, b_ref, o_ref, acc_ref, *, nk):
    k = pl.program_id(2)

    @pl.when(k == 0)
    def _():
        acc_ref[...] = jnp.zeros_like(acc_ref)

    acc_ref[...] += jnp.dot(a_ref[...], b_ref[...], preferred_element_type=F32)

    @pl.when(k == nk - 1)
    def _():
        o_ref[...] = acc_ref[...].astype(o_ref.dtype)


def _pick_block(n, target):
    if n <= target:
        return n
    best = 128
    for cand in range(128, target + 1, 128):
        if n % cand == 0:
            best = cand
    assert n % best == 0
    return best


def _wspec(w, layer, block, index_map):
    if w.ndim == 2:
        return pl.BlockSpec(block, index_map)
    return pl.BlockSpec((None,) + tuple(block), lambda *g: (layer,) + tuple(index_map(*g)))


def matmul(a, b, out_dtype, bm=1024, bn=1024, bk=None, layer=None):
    m, kdim = a.shape
    n = b.shape[-1]
    bm, bn = _pick_block(m, bm), _pick_block(n, bn)
    bk = kdim if bk is None else bk
    assert kdim % bk == 0
    nk = kdim // bk
    if nk == 1:
        return pl.pallas_call(
            _matmul_kernel,
            out_shape=jax.ShapeDtypeStruct((m, n), out_dtype),
            grid=(pl.cdiv(m, bm), pl.cdiv(n, bn)),
            in_specs=[pl.BlockSpec((bm, kdim), lambda i, j: (i, 0)),
                      _wspec(b, layer, (kdim, bn), lambda i, j: (0, j))],
            out_specs=pl.BlockSpec((bm, bn), lambda i, j: (i, j)),
            compiler_params=_params("parallel", "parallel"), name="matmul",
        )(a, b)
    return pl.pallas_call(
        functools.partial(_matmul_acc_kernel, nk=nk),
        out_shape=jax.ShapeDtypeStruct((m, n), out_dtype),
        grid=(pl.cdiv(m, bm), pl.cdiv(n, bn), nk),
        in_specs=[pl.BlockSpec((bm, bk), lambda i, j, k: (i, k)),
                  _wspec(b, layer, (bk, bn), lambda i, j, k: (k, j))],
        out_specs=pl.BlockSpec((bm, bn), lambda i, j, k: (i, j)),
        scratch_shapes=[pltpu.VMEM((bm, bn), F32)],
        compiler_params=_params("parallel", "parallel", "arbitrary"), name="matmul_acc",
    )(a, b)


def _matmul_addnorm_kernel(a_ref, w_ref, x_ref, gpost_ref, gnext_ref, xo_ref, ho_ref):
    y = jnp.dot(a_ref[...], w_ref[...], preferred_element_type=F32)
    xn = x_ref[...] + _rms(y, gpost_ref[...])
    xo_ref[...] = xn
    ho_ref[...] = _rms(xn, gnext_ref[...]).astype(ho_ref.dtype)


def matmul_addnorm(a, w, x, g_post, g_next, bm=256, layer=None):
    m, kdim = a.shape
    d = w.shape[-1]
    bm = _pick_block(m, bm)
    row = pl.BlockSpec((bm, d), lambda i: (i, 0))
    vec = pl.BlockSpec((1, d), lambda i: (0, 0))
    return pl.pallas_call(
        _matmul_addnorm_kernel,
        out_shape=(jax.ShapeDtypeStruct((m, d), F32), jax.ShapeDtypeStruct((m, d), BF16)),
        grid=(m // bm,),
        in_specs=[pl.BlockSpec((bm, kdim), lambda i: (i, 0)), _wspec(w, layer, (kdim, d), lambda i: (0, 0)),
                  row, vec, vec],
        out_specs=(row, row),
        compiler_params=_params("parallel"), name="matmul_addnorm",
    )(a, w, x, g_post.reshape(1, d), g_next.reshape(1, d))


def _norm_matmul_kernel(x_ref, g_ref, w_ref, *rest, rope_heads):
    o_ref = rest[-1]
    xn = _rms(x_ref[...], g_ref[...]).astype(BF16)
    y = jnp.dot(xn, w_ref[...], preferred_element_type=F32)
    if not rope_heads:
        o_ref[...] = y.astype(o_ref.dtype)
        return
    cs, sn = rest[0][...], rest[1][...]
    for hd in range(rope_heads):
        c0 = hd * MLA_QK_PAD
        x = y[:, c0 + MLA_NOPE:c0 + MLA_QK_PAD]
        o_ref[:, c0:c0 + MLA_NOPE] = y[:, c0:c0 + MLA_NOPE].astype(o_ref.dtype)
        o_ref[:, c0 + MLA_NOPE:c0 + MLA_QK_PAD] = (x * cs + pltpu.roll(x, MLA_ROPE, axis=1) * sn).astype(o_ref.dtype)


def norm_matmul(x_arr, col_block, kdim, g, w, rope=None, bm=512, layer=None):
    m = x_arr.shape[0]
    n = w.shape[-1]
    bm = _pick_block(m, bm)
    in_specs = [pl.BlockSpec((bm, kdim), lambda i: (i, col_block)),
                pl.BlockSpec((1, kdim), lambda i: (0, 0)),
                _wspec(w, layer, (kdim, n), lambda i: (0, 0))]
    args = [x_arr, g.reshape(1, kdim).astype(F32), w]
    if rope is not None:
        in_specs += [pl.BlockSpec((bm, HEAD_DIM), lambda i: (i, 0))] * 2
        args += [rope[0], rope[1]]
    return pl.pallas_call(
        functools.partial(_norm_matmul_kernel, rope_heads=0 if rope is None else rope[2]),
        out_shape=jax.ShapeDtypeStruct((m, n), BF16),
        grid=(m // bm,), in_specs=in_specs,
        out_specs=pl.BlockSpec((bm, n), lambda i: (i, 0)),
        compiler_params=_params("parallel"), name="norm_matmul",
    )(*args)


def _ffn_up_kernel(h_ref, wg_ref, wu_ref, cw_ref, cb_ref, *rest, tiles_per_seq):
    o_ref, carry_ref = rest[-2:]
    i = pl.program_id(1)
    bm = h_ref.shape[0]
    bn = o_ref.shape[1]

    @pl.when(i % tiles_per_seq == 0)
    def _():
        carry_ref[...] = jnp.zeros_like(carry_ref)

    row8 = lax.broadcasted_iota(jnp.int32, (8, 1), 0)
    for c0 in range(0, bn, FFN_CHUNK):
        cols = slice(c0, c0 + FFN_CHUNK)
        gate = jnp.dot(h_ref[...], wg_ref[:, cols], preferred_element_type=F32)
        up = jnp.dot(h_ref[...], wu_ref[:, cols], preferred_element_type=F32)
        prev = carry_ref[:, cols]
        carry_ref[:, cols] = gate[bm - 8:, :]
        g1 = pltpu.roll(gate, 1, axis=0)
        g2 = pltpu.roll(gate, 2, axis=0)
        p1 = pltpu.roll(prev, 1, axis=0)
        p2 = pltpu.roll(prev, 2, axis=0)
        g1 = jnp.concatenate([jnp.where(row8 < 1, p1, g1[:8]), g1[8:]], axis=0)
        g2 = jnp.concatenate([jnp.where(row8 < 2, p2, g2[:8]), g2[8:]], axis=0)
        cw = cw_ref[:, cols]
        conv = cw[0:1] * g2 + cw[1:2] * g1 + cw[2:3] * gate + cb_ref[:, cols]
        o_ref[:, cols] = (jax.nn.gelu(conv, approximate=True) * up).astype(o_ref.dtype)


def _ffn_up_call(h, wg, wu, conv_w, conv_b, seq, out, col0, n_cols, bm, bn, layer):
    m, d = h.shape
    f = wg.shape[-1]
    assert n_cols % bn == 0 and col0 % bn == 0 and bn % FFN_CHUNK == 0
    jb = col0 // bn
    in_specs = [pl.BlockSpec((bm, d), lambda j, i: (i, 0)),
                _wspec(wg, layer, (d, bn), lambda j, i: (0, jb + j)),
                _wspec(wu, layer, (d, bn), lambda j, i: (0, jb + j)),
                pl.BlockSpec((3, bn), lambda j, i: (0, jb + j)),
                pl.BlockSpec((1, bn), lambda j, i: (0, jb + j))]
    args = [h, wg, wu, conv_w, conv_b]
    aliases = {}
    if out is not None:
        in_specs.append(pl.BlockSpec(memory_space=pl.ANY))
        args.append(out)
        aliases = {5: 0}
    return pl.pallas_call(
        functools.partial(_ffn_up_kernel, tiles_per_seq=seq // bm),
        out_shape=jax.ShapeDtypeStruct((m, f), BF16),
        grid=(n_cols // bn, m // bm),
        in_specs=in_specs,
        out_specs=pl.BlockSpec((bm, bn), lambda j, i: (i, jb + j)),
        scratch_shapes=[pltpu.VMEM((8, bn), F32)],
        input_output_aliases=aliases,
        compiler_params=_params("arbitrary", "arbitrary"), name="ffn_up",
    )(*args)


def ffn_up(h, wg, wu, conv_w, conv_b, seq, bm=1024, bn=512, layer=None):
    m, d = h.shape
    f = wg.shape[-1]
    bm = _pick_block(seq, bm)
    assert seq % bm == 0 and m % seq == 0 and bm % 8 == 0 and f % FFN_CHUNK == 0
    conv_w = conv_w.astype(F32)
    conv_b = conv_b.reshape(1, f).astype(F32)
    bn = min(bn, f)
    main = (f // bn) * bn
    out = _ffn_up_call(h, wg, wu, conv_w, conv_b, seq, None, 0, main, bm, bn, layer)
    if main < f:
        bn_tail = math.gcd(main, f - main)
        out = _ffn_up_call(h, wg, wu, conv_w, conv_b, seq, out, main, f - main, bm, bn_tail, layer)
    return out


def _qk(q, k):
    return lax.dot_general(q, k, (((1,), (1,)), ((), ())), preferred_element_type=F32)


def _lanes(x, n):
    return x if n == HEAD_DIM else jnp.concatenate([x] * (n // HEAD_DIM), axis=1)


def _rows(ref, j, bk):
    return ref[pl.ds(pl.multiple_of(j * bk, bk), bk), :]


def _causal_mask(bq, bk, col_off, strict=False):
    row = lax.broadcasted_iota(jnp.int32, (bq, bk), 0)
    col = lax.broadcasted_iota(jnp.int32, (bq, bk), 1) + col_off
    return col < row if strict else col <= row


def _walk_pairs(pair, n_pairs):
    def quad(t, c):
        pair(4 * t)
        pair(4 * t + 2)
        return c

    n_pairs = jnp.maximum(n_pairs, 0)
    lax.fori_loop(0, n_pairs // 2, quad, 0)

    @pl.when(n_pairs % 2 == 1)
    def _():
        pair(2 * (n_pairs - 1))


def _softmax_update(s_ref, v_aug, m_ref, acc_ref, bias=None, mask_off=None, row_start=0):
    bq, bk = s_ref.shape
    live = slice(row_start, bq)
    if bias is None:
        s = s_ref[live, :]
    else:
        s = jnp.concatenate([s_ref[r0:r0 + bk, :] + bias[r0 // bk] for r0 in range(row_start, bq, bk)], axis=0)
    if mask_off is not None:
        row = lax.broadcasted_iota(jnp.int32, s.shape, 0) + row_start
        col = lax.broadcasted_iota(jnp.int32, s.shape, 1) + mask_off
        s = jnp.where(col <= row, s, NEG_BIG)
    m_prev = m_ref[live, :]
    m_new = jnp.maximum(m_prev, jnp.max(s, axis=-1, keepdims=True))
    alpha = jnp.exp2(m_prev - m_new)
    p = jnp.exp2(s - _lanes(m_new, bk))
    pv = jnp.dot(p.astype(BF16), v_aug, preferred_element_type=F32)
    acc_ref[live, :] = _lanes(alpha, 2 * HEAD_DIM) * acc_ref[live, :] + pv
    m_ref[live, :] = m_new


def _init_stats(m_ref, acc_ref):
    m_ref[...] = jnp.full_like(m_ref, NEG_BIG)
    acc_ref[...] = jnp.zeros_like(acc_ref)


def _normalised(acc_ref):
    acc = acc_ref[...]
    return acc[:, :HEAD_DIM] / acc[:, HEAD_DIM:]


def _fill_v_aug(vaug_ref, v_ref):
    vaug_ref[:, :HEAD_DIM] = v_ref[0]
    vaug_ref[:, HEAD_DIM:] = jnp.ones((vaug_ref.shape[0], HEAD_DIM), BF16)


def _softmax_attn_kernel(q_ref, k_ref, v_ref, g_ref, kx_ref, _, o_ref, kaug_ref, vaug_ref, m_ref, acc_ref,
                         sa_ref, sb_ref, *, fox, bq, bk):
    i = pl.program_id(2)

    @pl.when(i == 0)
    def _():
        kaug_ref[:, :HEAD_DIM] = k_ref[0]
        kaug_ref[:, HEAD_DIM:] = kx_ref[0, 0]
        _fill_v_aug(vaug_ref, v_ref)

    q = q_ref[0]
    if fox:
        lane = lax.broadcasted_iota(jnp.int32, (bq, HEAD_DIM), 1)
        q = jnp.concatenate([q, jnp.where(lane < 3, 1.0, 0.0).astype(BF16)], axis=1)
    _init_stats(m_ref, acc_ref)

    def logits(s_ref, j, row_start=0):
        s_ref[row_start:, :] = _qk(q[row_start:], _rows(kaug_ref, j, bk))

    def update(s_ref, j, **kw):
        _softmax_update(s_ref, _rows(vaug_ref, j, bk), m_ref, acc_ref, **kw)

    logits(sa_ref, 0)

    def pair(j):
        logits(sb_ref, j + 1)
        update(sa_ref, j)
        logits(sa_ref, j + 2)
        update(sb_ref, j + 1)

    _walk_pairs(pair, i)
    logits(sb_ref, 2 * i + 1, row_start=bk)
    update(sa_ref, 2 * i, mask_off=0)
    update(sb_ref, 2 * i + 1, mask_off=bk, row_start=bk)
    o_ref[0] = _rms(_normalised(acc_ref), g_ref[0]).astype(o_ref.dtype)


def _diff_attn_kernel(q_ref, k_ref, v_ref, g_ref, bdiag_ref, bsub_ref, far_ref, lam_ref, _, o_ref,
                      vaug_ref, m1_ref, a1_ref, m2_ref, a2_ref, s1a_ref, s2a_ref, s1b_ref, s2b_ref,
                      *, bq, bk, out_scale):
    i = pl.program_id(2)

    @pl.when(i == 0)
    def _():
        _fill_v_aug(vaug_ref, v_ref)

    q = q_ref[0]
    half = lax.broadcasted_iota(jnp.int32, q.shape, 1) < (HEAD_DIM // 2)
    zero = jnp.zeros_like(q)
    q1 = jnp.where(half, q, zero)
    q2 = jnp.where(half, zero, q)
    _init_stats(m1_ref, a1_ref)
    _init_stats(m2_ref, a2_ref)
    k_ref = k_ref.at[0]
    buf_a = (s1a_ref, s2a_ref)
    buf_b = (s1b_ref, s2b_ref)

    def logits(buf, j, row_start=0):
        k = _rows(k_ref, j, bk)
        buf[0][row_start:, :] = _qk(q1[row_start:], k)
        buf[1][row_start:, :] = _qk(q2[row_start:], k)

    far = _lanes(far_ref[0], bk)

    def update(buf, j, bias_top, bias_bottom, **kw):
        v = _rows(vaug_ref, j, bk)
        bias = None if bias_top is None and bias_bottom is None else (bias_top, bias_bottom)
        for s_ref, m_ref, a_ref in ((buf[0], m1_ref, a1_ref), (buf[1], m2_ref, a2_ref)):
            _softmax_update(s_ref, v, m_ref, a_ref, bias=bias, **kw)

    logits(buf_a, 0)

    def pair(j):
        logits(buf_b, j + 1)
        update(buf_a, j, None, None)
        logits(buf_a, j + 2)
        update(buf_b, j + 1, None, None)

    _walk_pairs(pair, i - 1)
    m1_ref[...] = m1_ref[...] + far_ref[0]
    m2_ref[...] = m2_ref[...] + far_ref[0]

    @pl.when(i >= 1)
    def _():
        logits(buf_b, 2 * i - 1)
        update(buf_a, 2 * i - 2, far, far)
        logits(buf_a, 2 * i)
        update(buf_b, 2 * i - 1, bsub_ref[0], far)

    logits(buf_b, 2 * i + 1, row_start=bk)
    update(buf_a, 2 * i, bdiag_ref[0], bsub_ref[0], mask_off=0)
    update(buf_b, 2 * i + 1, None, bdiag_ref[0], mask_off=bk, row_start=bk)
    o = _normalised(a1_ref) - lam_ref[...] * _normalised(a2_ref)
    o_ref[0] = (_rms(o, g_ref[0]) * out_scale).astype(o_ref.dtype)


def _sb_attn_kernel(q_ref, k_ref, v_ref, g_ref, u_ref, _, o_ref, r_ref, acc_ref, za_ref, zb_ref, *, bq, bk):
    i = pl.program_id(2)
    q = q_ref[0]
    k_ref = k_ref.at[0]
    v_ref = v_ref.at[0]
    r_ref[...] = jnp.zeros_like(r_ref)
    acc_ref[...] = jnp.zeros_like(acc_ref)
    u = u_ref[...]
    n_groups = bk // SB_GROUP

    def logits(z_ref, j):
        z_ref[...] = _qk(q, _rows(k_ref, jnp.maximum(j, 0), bk))

    def step(z_ref, j, mask_off=None):
        z = z_ref[...]
        r = r_ref[...]
        parts = [None] * n_groups
        for g in reversed(range(n_groups)):
            zg = jnp.minimum(z[:, g * SB_GROUP:(g + 1) * SB_GROUP], SB_LOGIT_CAP)
            sp = jnp.log2(1.0 + jnp.exp2(zg))
            if mask_off is not None:
                strict = _causal_mask(bq, SB_GROUP, mask_off + g * SB_GROUP, strict=True)
                sp = jnp.where(strict, sp, 0.0)
            between = jnp.dot(sp.astype(BF16), u, preferred_element_type=F32)
            a = jnp.exp2((zg - sp) + between + _lanes(r, SB_GROUP))
            if mask_off is not None:
                a = jnp.where(strict, a, 0.0)
            parts[g] = a.astype(BF16)
            r = r - jnp.sum(sp, axis=-1, keepdims=True)
        r_ref[...] = r
        acc_ref[...] += jnp.dot(jnp.concatenate(parts, axis=1), _rows(v_ref, j, bk),
                                preferred_element_type=F32)

    logits(za_ref, 2 * i + 1)
    logits(zb_ref, 2 * i)
    step(za_ref, 2 * i + 1, mask_off=bk)
    logits(za_ref, 2 * i - 1)
    step(zb_ref, 2 * i, mask_off=0)

    def pair(jj):
        j = 2 * i - 1 - jj
        logits(zb_ref, j - 1)
        step(za_ref, j)
        logits(za_ref, j - 2)
        step(zb_ref, j - 1)

    _walk_pairs(pair, i)

    o_ref[0] = _rms(acc_ref[...], g_ref[0]).astype(o_ref.dtype)


def _attn_call(kernel, o_buf, head_off, n_heads, q_arr, k_arr, v_arr, gain, extra_in, extra_specs, scratch,
               dq, q_col, k_col, v_col, name):
    nb, seq, _ = q_arr.shape
    bq, bk = _attn_blocks(seq)
    in_specs = [pl.BlockSpec((1, bq, dq), lambda b, h, i: (b, i, q_col(h))),
                pl.BlockSpec((1, seq, HEAD_DIM), lambda b, h, i: (b, 0, k_col(h))),
                pl.BlockSpec((1, seq, HEAD_DIM), lambda b, h, i: (b, 0, v_col(h))),
                pl.BlockSpec((1, 1, HEAD_DIM), lambda b, h, i: (h, 0, 0))]
    in_specs += extra_specs + [pl.BlockSpec(memory_space=pl.ANY)]
    n_in = len(in_specs)
    return pl.pallas_call(
        functools.partial(kernel, bq=bq, bk=bk),
        out_shape=jax.ShapeDtypeStruct(o_buf.shape, o_buf.dtype),
        grid=(nb, n_heads, seq // bq),
        in_specs=in_specs,
        out_specs=pl.BlockSpec((1, bq, HEAD_DIM), lambda b, h, i: (b, i, head_off + h)),
        scratch_shapes=scratch,
        input_output_aliases={n_in - 1: 0},
        compiler_params=_params("arbitrary", "arbitrary", "arbitrary", flags=ATTN_FLAGS), name=name,
    )(q_arr, k_arr, v_arr, gain.reshape(n_heads, 1, HEAD_DIM).astype(F32), *extra_in, o_buf)


def _attn_blocks(seq):
    bk = min(ATTN_KEY_BLOCK, seq // 2)
    assert seq % (2 * bk) == 0 and bk % SB_GROUP == 0
    return 2 * bk, bk


def _stat_scratch(bq, n):
    out = []
    for _ in range(n):
        out += [pltpu.VMEM((bq, HEAD_DIM), F32), pltpu.VMEM((bq, 2 * HEAD_DIM), F32)]
    return out


def _logit_scratch(bq, bk, n):
    return [pltpu.VMEM((bq, bk), F32)] * n


def _group(off):
    return lambda h: off + h


def sb_attention(o_buf, head_off, qkv, gain, q_off, k_off, v_off):
    bq, bk = _attn_blocks(qkv.shape[1])
    u = -jnp.tril(jnp.ones((SB_GROUP, SB_GROUP), F32), -1).astype(BF16)
    return _attn_call(_sb_attn_kernel, o_buf, head_off, N_SB, qkv, qkv, qkv, gain, [u],
                      [pl.BlockSpec((SB_GROUP, SB_GROUP), lambda b, h, i: (0, 0))],
                      [pltpu.VMEM((bq, HEAD_DIM), F32), pltpu.VMEM((bq, HEAD_DIM), F32)]
                      + _logit_scratch(bq, bk, 2),
                      HEAD_DIM, _group(q_off), _group(k_off), _group(v_off), "sb_attention")


def _toeplitz(g, n):
    nh = g.shape[0]
    t = jnp.broadcast_to(g[:, None, :], (nh, n, 2 * n - 1))
    t = jnp.pad(t, ((0, 0), (0, 0), (0, 1))).reshape(nh, 2 * n * n)[:, :n * (2 * n - 1)]
    return t.reshape(nh, n, 2 * n - 1)[:, :, n - 1:]


def diff_attention(o_buf, head_off, qkv, gain, t5_table, lam, out_scale, q_off, k_off, v_off):
    seq = qkv.shape[1]
    bq, bk = _attn_blocks(seq)
    assert bk >= MAX_DISTANCE
    table = t5_table.astype(F32) * LOG2E
    f_table = table[_t5_bucket(jnp.arange(2 * bk, dtype=jnp.int32))].T
    x = jnp.arange(2 * bk - 1)
    b_diag = _toeplitz(f_table[:, jnp.maximum(bk - 1 - x, 0)], bk)
    b_sub = _toeplitz(f_table[:, 2 * bk - 1 - x], bk)
    far = jnp.broadcast_to(table[N_BUCKETS - 1][:, None, None], (N_DIFF, 1, HEAD_DIM))
    lam_row = jnp.broadcast_to(lam.astype(F32).reshape(1, 1), (1, HEAD_DIM))
    blk = pl.BlockSpec((1, bk, bk), lambda b, h, i: (h, 0, 0))
    return _attn_call(functools.partial(_diff_attn_kernel, out_scale=out_scale), o_buf, head_off, N_DIFF,
                      qkv, qkv, qkv, gain, [b_diag, b_sub, far, lam_row],
                      [blk, blk, pl.BlockSpec((1, 1, HEAD_DIM), lambda b, h, i: (h, 0, 0)),
                       pl.BlockSpec((1, HEAD_DIM), lambda b, h, i: (0, 0))],
                      [pltpu.VMEM((seq, 2 * HEAD_DIM), BF16)] + _stat_scratch(bq, 2)
                      + _logit_scratch(bq, bk, 4),
                      HEAD_DIM, _group(q_off), _group(k_off), _group(v_off), "diff_attention")


def _wide_softmax_attention(o_buf, head_off, fox, n_heads, q_arr, dq, q_col, kv_arr, k_col, v_col, kx, kx_spec,
                            gain, name):
    seq = q_arr.shape[1]
    bq, bk = _attn_blocks(seq)
    scratch = [pltpu.VMEM((seq, 2 * HEAD_DIM), BF16), pltpu.VMEM((seq, 2 * HEAD_DIM), BF16)]
    return _attn_call(functools.partial(_softmax_attn_kernel, fox=fox), o_buf, head_off, n_heads,
                      q_arr, kv_arr, kv_arr, gain, [kx], [kx_spec],
                      scratch + _stat_scratch(bq, 1) + _logit_scratch(bq, bk, 2),
                      dq, q_col, k_col, v_col, name)


def mla_attention(o_buf, head_off, q_cat, kv, k_rope_pad, gain):
    nb, seq, _ = q_cat.shape
    spec = pl.BlockSpec((1, 1, seq, HEAD_DIM), lambda b, h, i: (b, 0, 0, 0))
    return _wide_softmax_attention(o_buf, head_off, False, N_MLA, q_cat, MLA_QK_PAD, lambda h: h,
                                   kv, lambda h: 2 * h, lambda h: 2 * h + 1,
                                   k_rope_pad.reshape(nb, 1, seq, HEAD_DIM), spec, gain, "mla_attention")


def fox_attention(o_buf, head_off, qkv, gain, c, q_off, k_off, v_off):
    nb, seq, _ = qkv.shape
    c2 = jnp.moveaxis(c, -1, 1) * LOG2E
    hi = c2.astype(BF16)
    mid = (c2 - hi.astype(F32)).astype(BF16)
    lo = (c2 - hi.astype(F32) - mid.astype(F32)).astype(BF16)
    kx = jnp.stack([-hi, -mid, -lo], axis=-1)
    kx = jnp.pad(kx, ((0, 0), (0, 0), (0, 0), (0, HEAD_DIM - 3)))
    spec = pl.BlockSpec((1, 1, seq, HEAD_DIM), lambda b, h, i: (b, h, 0, 0))
    return _wide_softmax_attention(o_buf, head_off, True, N_FOX, qkv, HEAD_DIM, _group(q_off),
                                   qkv, _group(k_off), _group(v_off), kx, spec, gain, "fox_attention")


def _xattn_kernel(q_ref, k_ref, v_ref, o_ref):
    for h in range(N_XHEADS):
        sl = slice(h * HEAD_DIM, (h + 1) * HEAD_DIM)
        s = _qk(q_ref[0, :, sl], k_ref[0, :, sl])
        p = jnp.exp(s - jnp.max(s, axis=-1, keepdims=True))
        o = jnp.dot(p.astype(BF16), v_ref[0, :, sl], preferred_element_type=F32)
        o_ref[0, :, sl] = (o / jnp.sum(p, axis=-1, keepdims=True)).astype(o_ref.dtype)


def cross_attention(q, k, v, bq=512):
    nb, seq, w = q.shape
    n_mem = k.shape[1]
    bq = min(bq, seq)
    return pl.pallas_call(
        _xattn_kernel,
        out_shape=jax.ShapeDtypeStruct((nb, seq, w), BF16),
        grid=(nb, seq // bq),
        in_specs=[pl.BlockSpec((1, bq, w), lambda b, i: (b, i, 0)),
                  pl.BlockSpec((1, n_mem, w), lambda b, i: (b, 0, 0)),
                  pl.BlockSpec((1, n_mem, w), lambda b, i: (b, 0, 0))],
        out_specs=pl.BlockSpec((1, bq, w), lambda b, i: (b, i, 0)),
        compiler_params=_params("parallel", "parallel"), name="cross_attention",
    )(q, k, v)


def _t5_bucket(rel):
    n = jnp.maximum(rel, 0)
    max_exact = N_BUCKETS // 2
    nf = jnp.maximum(n, 1).astype(F32)
    large = max_exact + (jnp.log(nf / max_exact) / math.log(MAX_DISTANCE / max_exact)
                         * (N_BUCKETS - max_exact)).astype(jnp.int32)
    large = jnp.minimum(large, N_BUCKETS - 1)
    return jnp.where(n < max_exact, n, large)


def _rope(x, cos, sin):
    x1, x2 = jnp.split(x, 2, axis=-1)
    return jnp.concatenate([x1 * cos - x2 * sin, x2 * cos + x1 * sin], axis=-1)


HW = N_SB * HEAD_DIM
IN_DF = 3 * HW
IN_CQ = 6 * HW
IN_CKV = IN_CQ + MLA_Q_RANK
IN_KR = IN_CKV + MLA_KV_RANK
IN_FX = IN_KR + MLA_ROPE
IN_FL = IN_FX + 3 * HW
SMALL_CKV = 2 * MLA_KV_RANK
SMALL_KR = SMALL_CKV + MLA_KV_RANK
SMALL_FL = SMALL_KR + HEAD_DIM
SMALL_N = SMALL_FL + HEAD_DIM


def _prep_w_in_kernel(w_ref, scale_ref, main_ref, small_ref):
    w = w_ref[...] * scale_ref[...]
    main_ref[:, :IN_CQ] = w[:, :IN_CQ].astype(BF16)
    main_ref[:, IN_CQ:] = w[:, IN_FX:IN_FL].astype(BF16)
    small_ref[...] = jnp.zeros_like(small_ref)
    small_ref[:, :MLA_Q_RANK] = w[:, IN_CQ:IN_CKV].astype(BF16)
    small_ref[:, SMALL_CKV:SMALL_KR] = w[:, IN_CKV:IN_KR].astype(BF16)
    small_ref[:, SMALL_KR:SMALL_KR + MLA_ROPE] = w[:, IN_KR:IN_FX].astype(BF16)
    small_ref[:, SMALL_FL:SMALL_FL + N_FOX] = w[:, IN_FL:IN_FL + N_FOX].astype(BF16)


def _prep_w_in(w_in, bd=128):
    nl, d, n_in = w_in.shape
    sc = HEAD_DIM ** -0.5 * LOG2E
    sc_df = (HEAD_DIM // 2) ** -0.5 * LOG2E
    col_scale = jnp.ones((1, n_in), F32)
    col_scale = col_scale.at[:, 0:HW].set(sc).at[:, IN_DF:IN_DF + HW].set(sc_df).at[:, IN_FX:IN_FX + HW].set(sc)
    bd = _pick_block(d, bd)
    return pl.pallas_call(
        _prep_w_in_kernel,
        out_shape=(jax.ShapeDtypeStruct((nl, d, 9 * HW), BF16), jax.ShapeDtypeStruct((nl, d, SMALL_N), BF16)),
        grid=(nl, d // bd),
        in_specs=[pl.BlockSpec((None, bd, n_in), lambda l, i: (l, i, 0)),
                  pl.BlockSpec((1, n_in), lambda l, i: (0, 0))],
        out_specs=(pl.BlockSpec((None, bd, 9 * HW), lambda l, i: (l, i, 0)),
                   pl.BlockSpec((None, bd, SMALL_N), lambda l, i: (l, i, 0))),
        compiler_params=_params("parallel", "parallel"), name="prep_w_in",
    )(w_in, col_scale)


def _mixer(h, nb, seq, cos, sin, t5_table, w_main, w_small, q_norm, w_uq, kv_norm, w_ukv, lam_vecs, b_forget,
           head_norm, w_out, layer_idx):
    sc_mla = (MLA_NOPE + MLA_ROPE) ** -0.5 * LOG2E
    qkv = matmul(h, w_main, BF16, layer=layer_idx).reshape(nb, seq, 9 * HW)
    small = matmul(h, w_small, F32, bm=512, bn=SMALL_N, layer=layer_idx)
    k_r = small[:, SMALL_KR:SMALL_KR + MLA_ROPE]
    f_logit = small[:, SMALL_FL:SMALL_FL + N_FOX]

    g_sb, g_df, g_mla, g_fx = jnp.split(head_norm, [N_SB, N_SB + N_DIFF, N_SB + N_DIFF + N_MLA], axis=0)
    o = jnp.zeros((nb, seq, (N_SB + N_DIFF + N_MLA + N_FOX) * HEAD_DIM), BF16)

    o = sb_attention(o, 0, qkv, g_sb, 0, N_SB, 2 * N_SB)

    lv = lam_vecs.astype(F32)
    lam_init = 0.8 - 0.6 * math.exp(-0.3 * layer_idx)
    lam = jnp.exp(jnp.sum(lv[0] * lv[1])) - jnp.exp(jnp.sum(lv[2] * lv[3])) + lam_init
    o = diff_attention(o, N_SB, qkv, g_df, t5_table, lam, 1.0 - lam_init, 3 * N_SB, 4 * N_SB, 5 * N_SB)

    wq = (w_uq * sc_mla).reshape(MLA_Q_RANK, N_MLA, MLA_NOPE + MLA_ROPE)
    w_r = wq[..., MLA_NOPE:]
    w_rot = jnp.concatenate([w_r[..., MLA_ROPE // 2:], w_r[..., :MLA_ROPE // 2]], axis=-1)
    wq = jnp.concatenate([wq, w_rot], axis=-1).reshape(MLA_Q_RANK, N_MLA * MLA_QK_PAD).astype(BF16)
    zeros = jnp.zeros_like(cos)
    cs = jnp.concatenate([cos, cos, zeros, zeros], axis=-1).reshape(nb * seq, HEAD_DIM)
    sn = jnp.concatenate([-sin, sin, zeros, zeros], axis=-1).reshape(nb * seq, HEAD_DIM)
    q_cat = norm_matmul(small, 0, MLA_Q_RANK, q_norm, wq, rope=(cs, sn, N_MLA))
    q_cat = q_cat.reshape(nb, seq, N_MLA * MLA_QK_PAD)
    kv = norm_matmul(small, SMALL_CKV // MLA_KV_RANK, MLA_KV_RANK, kv_norm, w_ukv.astype(BF16))
    kv = kv.reshape(nb, seq, N_MLA * 2 * HEAD_DIM)
    k_rope = _rope(k_r.reshape(nb, seq, MLA_ROPE), cos, sin)
    k_rope_pad = jnp.pad(k_rope, ((0, 0), (0, 0), (0, HEAD_DIM - MLA_ROPE))).astype(BF16)
    o = mla_attention(o, N_SB + N_DIFF, q_cat, kv, k_rope_pad, g_mla)

    log_f = jax.nn.log_sigmoid(f_logit.reshape(nb, seq, N_FOX) + b_forget.astype(F32))
    c = lax.cumsum(log_f, axis=1)
    o = fox_attention(o, N_SB + N_DIFF + N_MLA, qkv, g_fx, c, 6 * N_SB, 7 * N_SB, 8 * N_SB)

    return matmul(o.reshape(nb * seq, -1), w_out, F32, layer=layer_idx)


def kernel(x, mem, positions, t5_table, mix_norm_pre, w_in, mla_q_norm, w_uq, mla_kv_norm, w_ukv, diff_lambda, b_forget, head_norm, w_out, mix_norm_post, xattn_norm_pre, mem_norm, w_xq, w_xk, w_xv, w_xo, xattn_norm_post, ffn_norm_pre, w_gate, w_up, conv_w, conv_b, w_down, ffn_norm_post):
    nb, seq, d = x.shape
    depth = w_in.shape[0]
    n_mem = mem.shape[1]
    d_ff = w_gate.shape[2]
    xw = N_XHEADS * HEAD_DIM
    inv = ROPE_THETA ** (-jnp.arange(0, MLA_ROPE, 2, dtype=F32) / MLA_ROPE)
    ang = positions.astype(F32)[..., None] * inv
    cos, sin = jnp.cos(ang), jnp.sin(ang)

    xf = x.reshape(nb * seq, d)
    memf = mem.reshape(nb * n_mem, d)
    w_main, w_small = _prep_w_in(w_in)
    w_out, w_xo = w_out.astype(BF16), w_xo.astype(BF16)
    w_gate, w_up, w_down = w_gate.astype(BF16), w_up.astype(BF16), w_down.astype(BF16)
    h = rmsnorm(xf, mix_norm_pre[0])
    for l in range(depth):
        y = _mixer(h, nb, seq, cos, sin, t5_table, w_main, w_small, mla_q_norm[l], w_uq[l], mla_kv_norm[l],
                   w_ukv[l], diff_lambda[l], b_forget[l], head_norm[l], w_out, l)
        xf, h = addnorm(y, xf, mix_norm_post[l], xattn_norm_pre[l])

        mem_n = rmsnorm(memf, mem_norm[l])
        q = matmul(h, (w_xq[l] * HEAD_DIM ** -0.5).astype(BF16), BF16).reshape(nb, seq, xw)
        k = matmul(mem_n, w_xk[l].astype(BF16), BF16).reshape(nb, n_mem, xw)
        v = matmul(mem_n, w_xv[l].astype(BF16), BF16).reshape(nb, n_mem, xw)
        o = cross_attention(q, k, v).reshape(nb * seq, xw)
        xf, h = matmul_addnorm(o, w_xo, xf, xattn_norm_post[l], ffn_norm_pre[l], layer=l)

        act = ffn_up(h, w_gate, w_up, conv_w[l], conv_b[l], seq, layer=l)
        y = matmul(act, w_down, F32, bm=512, bn=512, layer=l)
        g_next = mix_norm_pre[l + 1] if l + 1 < depth else None
        xf, h = addnorm(y, xf, ffn_norm_post[l], g_next)
    return xf.reshape(nb, seq, d)
```

```python
import functools
import math

import jax
import jax.numpy as jnp
from jax import lax
from jax.experimental import pallas as pl
from jax.experimental.pallas import tpu as pltpu

F32 = jnp.float32
BF16 = jnp.bfloat16

HEAD_DIM = 128
N_SB = 8
N_DIFF = 8
N_MLA = 8
N_FOX = 8
MLA_Q_RANK = 768
MLA_KV_RANK = 512
MLA_NOPE = 128
MLA_ROPE = 64
MLA_QK_PAD = 256
N_XHEADS = 4
N_BUCKETS = 32
MAX_DISTANCE = 128
ROPE_THETA = 10000.0
EPS = 1e-6
NEG_BIG = -1e30
LOG2E = math.log2(math.e)

V7X_VMEM_LIMIT_BYTES = 56 * 1024 * 1024
ATTN_KEY_BLOCK = 512
SB_GROUP = 256
SB_LOGIT_CAP = 120.0
FFN_CHUNK = 256


def _params(*sem, flags=None):
    return pltpu.CompilerParams(dimension_semantics=sem, vmem_limit_bytes=V7X_VMEM_LIMIT_BYTES, flags=flags)


ATTN_FLAGS = None


def _rms(x, g):
    return x * lax.rsqrt(jnp.mean(x * x, axis=-1, keepdims=True) + EPS) * g


def _rmsnorm_kernel(x_ref, g_ref, o_ref):
    o_ref[...] = _rms(x_ref[...].astype(F32), g_ref[...]).astype(o_ref.dtype)


def rmsnorm(x, g, out_dtype=BF16, bm=512):
    m, d = x.shape
    bm = min(bm, m)
    return pl.pallas_call(
        _rmsnorm_kernel,
        out_shape=jax.ShapeDtypeStruct((m, d), out_dtype),
        grid=(pl.cdiv(m, bm),),
        in_specs=[pl.BlockSpec((bm, d), lambda i: (i, 0)), pl.BlockSpec((1, d), lambda i: (0, 0))],
        out_specs=pl.BlockSpec((bm, d), lambda i: (i, 0)),
        compiler_params=_params("parallel"),
        name="rmsnorm",
    )(x, g.reshape(1, d).astype(F32))


def _addnorm_kernel(y_ref, x_ref, gpost_ref, gnext_ref, xo_ref, ho_ref):
    xn = x_ref[...] + _rms(y_ref[...], gpost_ref[...])
    xo_ref[...] = xn
    ho_ref[...] = _rms(xn, gnext_ref[...]).astype(ho_ref.dtype)


def _addnorm_last_kernel(y_ref, x_ref, gpost_ref, xo_ref):
    xo_ref[...] = x_ref[...] + _rms(y_ref[...], gpost_ref[...])


def addnorm(y, x, g_post, g_next=None, bm=256):
    m, d = x.shape
    bm = min(bm, m)
    row = pl.BlockSpec((bm, d), lambda i: (i, 0))
    vec = pl.BlockSpec((1, d), lambda i: (0, 0))
    if g_next is None:
        return pl.pallas_call(
            _addnorm_last_kernel,
            out_shape=jax.ShapeDtypeStruct((m, d), F32),
            grid=(pl.cdiv(m, bm),), in_specs=[row, row, vec], out_specs=row,
            compiler_params=_params("parallel"), name="addnorm_last",
        )(y, x, g_post.reshape(1, d)), None
    return pl.pallas_call(
        _addnorm_kernel,
        out_shape=(jax.ShapeDtypeStruct((m, d), F32), jax.ShapeDtypeStruct((m, d), BF16)),
        grid=(pl.cdiv(m, bm),), in_specs=[row, row, vec, vec], out_specs=(row, row),
        compiler_params=_params("parallel"), name="addnorm",
    )(y, x, g_post.reshape(1, d), g_next.reshape(1, d))


def _matmul_kernel(a_ref, b_ref, o_ref):
    o_ref[...] = jnp.dot(a_ref[...], b_ref[...], preferred_element_type=F32).astype(o_ref.dtype)


def _matmul_acc_kernel(a_ref, b_ref, o_ref, acc_ref, *, nk):
    k = pl.program_id(2)

    @pl.when(k == 0)
    def _():
        acc_ref[...] = jnp.zeros_like(acc_ref)

    acc_ref[...] += jnp.dot(a_ref[...], b_ref[...], preferred_element_type=F32)

    @pl.when(k == nk - 1)
    def _():
        o_ref[...] = acc_ref[...].astype(o_ref.dtype)


def _pick_block(n, target):
    if n <= target:
        return n
    best = 128
    for cand in range(128, target + 1, 128):
        if n % cand == 0:
            best = cand
    assert n % best == 0
    return best


def _wspec(w, layer, block, index_map):
    if w.ndim == 2:
        return pl.BlockSpec(block, index_map)
    return pl.BlockSpec((None,) + tuple(block), lambda *g: (layer,) + tuple(index_map(*g)))


def matmul(a, b, out_dtype, bm=1024, bn=1024, bk=None, layer=None):
    m, kdim = a.shape
    n = b.shape[-1]
    bm, bn = _pick_block(m, bm), _pick_block(n, bn)
    bk = kdim if bk is None else bk
    assert kdim % bk == 0
    nk = kdim // bk
    if nk == 1:
        return pl.pallas_call(
            _matmul_kernel,
            out_shape=jax.ShapeDtypeStruct((m, n), out_dtype),
            grid=(pl.cdiv(m, bm), pl.cdiv(n, bn)),
            in_specs=[pl.BlockSpec((bm, kdim), lambda i, j: (i, 0)),
                      _wspec(b, layer, (kdim, bn), lambda i, j: (0, j))],
            out_specs=pl.BlockSpec((bm, bn), lambda i, j: (i, j)),
            compiler_params=_params("parallel", "parallel"), name="matmul",
        )(a, b)
    return pl.pallas_call(
        functools.partial(_matmul_acc_kernel, nk=nk),
        out_shape=jax.ShapeDtypeStruct((m, n), out_dtype),
        grid=(pl.cdiv(m, bm), pl.cdiv(n, bn), nk),
        in_specs=[pl.BlockSpec((bm, bk), lambda i, j, k: (i, k)),
                  _wspec(b, layer, (bk, bn), lambda i, j, k: (k, j))],
        out_specs=pl.BlockSpec((bm, bn), lambda i, j, k: (i, j)),
        scratch_shapes=[pltpu.VMEM((bm, bn), F32)],
        compiler_params=_params("parallel", "parallel", "arbitrary"), name="matmul_acc",
    )(a, b)


def _matmul_addnorm_kernel(a_ref, w_ref, x_ref, gpost_ref, gnext_ref, xo_ref, ho_ref):
    y = jnp.dot(a_ref[...], w_ref[...], preferred_element_type=F32)
    xn = x_ref[...] + _rms(y, gpost_ref[...])
    xo_ref[...] = xn
    ho_ref[...] = _rms(xn, gnext_ref[...]).astype(ho_ref.dtype)


def matmul_addnorm(a, w, x, g_post, g_next, bm=256, layer=None):
    m, kdim = a.shape
    d = w.shape[-1]
    bm = _pick_block(m, bm)
    row = pl.BlockSpec((bm, d), lambda i: (i, 0))
    vec = pl.BlockSpec((1, d), lambda i: (0, 0))
    return pl.pallas_call(
        _matmul_addnorm_kernel,
        out_shape=(jax.ShapeDtypeStruct((m, d), F32), jax.ShapeDtypeStruct((m, d), BF16)),
        grid=(m // bm,),
        in_specs=[pl.BlockSpec((bm, kdim), lambda i: (i, 0)), _wspec(w, layer, (kdim, d), lambda i: (0, 0)),
                  row, vec, vec],
        out_specs=(row, row),
        compiler_params=_params("parallel"), name="matmul_addnorm",
    )(a, w, x, g_post.reshape(1, d), g_next.reshape(1, d))


def _norm_matmul_kernel(x_ref, g_ref, w_ref, *rest, rope_heads):
    o_ref = rest[-1]
    xn = _rms(x_ref[...], g_ref[...]).astype(BF16)
    y = jnp.dot(xn, w_ref[...], preferred_element_type=F32)
    if not rope_heads:
        o_ref[...] = y.astype(o_ref.dtype)
        return
    cs, sn = rest[0][...], rest[1][...]
    for hd in range(rope_heads):
        c0 = hd * MLA_QK_PAD
        x = y[:, c0 + MLA_NOPE:c0 + MLA_QK_PAD]
        o_ref[:, c0:c0 + MLA_NOPE] = y[:, c0:c0 + MLA_NOPE].astype(o_ref.dtype)
        o_ref[:, c0 + MLA_NOPE:c0 + MLA_QK_PAD] = (x * cs + pltpu.roll(x, MLA_ROPE, axis=1) * sn).astype(o_ref.dtype)


def norm_matmul(x_arr, col_block, kdim, g, w, rope=None, bm=512, layer=None):
    m = x_arr.shape[0]
    n = w.shape[-1]
    bm = _pick_block(m, bm)
    in_specs = [pl.BlockSpec((bm, kdim), lambda i: (i, col_block)),
                pl.BlockSpec((1, kdim), lambda i: (0, 0)),
                _wspec(w, layer, (kdim, n), lambda i: (0, 0))]
    args = [x_arr, g.reshape(1, kdim).astype(F32), w]
    if rope is not None:
        in_specs += [pl.BlockSpec((bm, HEAD_DIM), lambda i: (i, 0))] * 2
        args += [rope[0], rope[1]]
    return pl.pallas_call(
        functools.partial(_norm_matmul_kernel, rope_heads=0 if rope is None else rope[2]),
        out_shape=jax.ShapeDtypeStruct((m, n), BF16),
        grid=(m // bm,), in_specs=in_specs,
        out_specs=pl.BlockSpec((bm, n), lambda i: (i, 0)),
        compiler_params=_params("parallel"), name="norm_matmul",
    )(*args)


def _ffn_up_kernel(h_ref, wg_ref, wu_ref, cw_ref, cb_ref, o_ref, carry_ref, *, tiles_per_seq):
    i = pl.program_id(1)
    bm = h_ref.shape[0]
    bn = o_ref.shape[1]

    @pl.when(i % tiles_per_seq == 0)
    def _():
        carry_ref[...] = jnp.zeros_like(carry_ref)

    row8 = lax.broadcasted_iota(jnp.int32, (8, 1), 0)
    for c0 in range(0, bn, FFN_CHUNK):
        cols = slice(c0, c0 + FFN_CHUNK)
        gate = jnp.dot(h_ref[...], wg_ref[:, cols], preferred_element_type=F32)
        up = jnp.dot(h_ref[...], wu_ref[:, cols], preferred_element_type=F32)
        prev = carry_ref[:, cols]
        carry_ref[:, cols] = gate[bm - 8:, :]
        g1 = pltpu.roll(gate, 1, axis=0)
        g2 = pltpu.roll(gate, 2, axis=0)
        p1 = pltpu.roll(prev, 1, axis=0)
        p2 = pltpu.roll(prev, 2, axis=0)
        g1 = jnp.concatenate([jnp.where(row8 < 1, p1, g1[:8]), g1[8:]], axis=0)
        g2 = jnp.concatenate([jnp.where(row8 < 2, p2, g2[:8]), g2[8:]], axis=0)
        cw = cw_ref[:, cols]
        conv = cw[0:1] * g2 + cw[1:2] * g1 + cw[2:3] * gate + cb_ref[:, cols]
        o_ref[:, cols] = (jax.nn.gelu(conv, approximate=True) * up).astype(o_ref.dtype)


def _ffn_up_call(h, wg, wu, conv_w, conv_b, seq, col0, n_cols, bm, bn, layer):
    m, d = h.shape
    assert n_cols % bn == 0 and col0 % bn == 0 and bn % FFN_CHUNK == 0
    jb = col0 // bn
    return pl.pallas_call(
        functools.partial(_ffn_up_kernel, tiles_per_seq=seq // bm),
        out_shape=jax.ShapeDtypeStruct((m, n_cols), BF16),
        grid=(n_cols // bn, m // bm),
        in_specs=[pl.BlockSpec((bm, d), lambda j, i: (i, 0)),
                  _wspec(wg, layer, (d, bn), lambda j, i: (0, jb + j)),
                  _wspec(wu, layer, (d, bn), lambda j, i: (0, jb + j)),
                  pl.BlockSpec((3, bn), lambda j, i: (0, jb + j)),
                  pl.BlockSpec((1, bn), lambda j, i: (0, jb + j))],
        out_specs=pl.BlockSpec((bm, bn), lambda j, i: (i, j)),
        scratch_shapes=[pltpu.VMEM((8, bn), F32)],
        compiler_params=_params("arbitrary", "arbitrary"), name="ffn_up",
    )(h, wg, wu, conv_w, conv_b)


def ffn_up(h, wg, wu, conv_w, conv_b, seq, bm=1024, bn=512, layer=None):
    m, d = h.shape
    f = wg.shape[-1]
    bm = _pick_block(seq, bm)
    assert seq % bm == 0 and m % seq == 0 and bm % 8 == 0 and f % FFN_CHUNK == 0
    conv_w = conv_w.astype(F32)
    conv_b = conv_b.reshape(1, f).astype(F32)
    bn = min(bn, f)
    n_main = (f // bn) * bn
    main = _ffn_up_call(h, wg, wu, conv_w, conv_b, seq, 0, n_main, bm, bn, layer)
    tail = None
    if n_main < f:
        bn_tail = math.gcd(n_main, f - n_main)
        tail = _ffn_up_call(h, wg, wu, conv_w, conv_b, seq, n_main, f - n_main, bm, bn_tail, layer)
    return main, tail


def _matmul2_kernel(a1_ref, a2_ref, w1_ref, w2_ref, o_ref):
    o_ref[...] = (jnp.dot(a1_ref[...], w1_ref[...], preferred_element_type=F32)
                  + jnp.dot(a2_ref[...], w2_ref[...], preferred_element_type=F32)).astype(o_ref.dtype)


def ffn_down(main, tail, w, layer, bm=512, bn=512):
    if tail is None:
        return matmul(main, w, F32, bm=bm, bn=bn, layer=layer)
    m, k1 = main.shape
    k2 = tail.shape[1]
    n = w.shape[-1]
    assert k1 % k2 == 0 and w.shape[-2] == k1 + k2
    bm, bn = _pick_block(m, bm), _pick_block(n, bn)
    return pl.pallas_call(
        _matmul2_kernel,
        out_shape=jax.ShapeDtypeStruct((m, n), F32),
        grid=(m // bm, n // bn),
        in_specs=[pl.BlockSpec((bm, k1), lambda i, j: (i, 0)),
                  pl.BlockSpec((bm, k2), lambda i, j: (i, 0)),
                  _wspec(w, layer, (k1, bn), lambda i, j: (0, j)),
                  _wspec(w, layer, (k2, bn), lambda i, j: (k1 // k2, j))],
        out_specs=pl.BlockSpec((bm, bn), lambda i, j: (i, j)),
        compiler_params=_params("parallel", "parallel"), name="ffn_down",
    )(main, tail, w, w)


def _qk(q, k):
    return lax.dot_general(q, k, (((1,), (1,)), ((), ())), preferred_element_type=F32)


def _lanes(x, n):
    return x if n == HEAD_DIM else jnp.concatenate([x] * (n // HEAD_DIM), axis=1)


def _rows(ref, j, bk):
    return ref[pl.ds(pl.multiple_of(j * bk, bk), bk), :]


def _causal_mask(bq, bk, col_off, strict=False):
    row = lax.broadcasted_iota(jnp.int32, (bq, bk), 0)
    col = lax.broadcasted_iota(jnp.int32, (bq, bk), 1) + col_off
    return col < row if strict else col <= row


def _walk_pairs(pair, n_pairs):
    def quad(t, c):
        pair(4 * t)
        pair(4 * t + 2)
        return c

    n_pairs = jnp.maximum(n_pairs, 0)
    lax.fori_loop(0, n_pairs // 2, quad, 0)

    @pl.when(n_pairs % 2 == 1)
    def _():
        pair(2 * (n_pairs - 1))


def _softmax_update(s_ref, v_aug, m_ref, acc_ref, bias=None, mask_off=None, row_start=0):
    bq, bk = s_ref.shape
    live = slice(row_start, bq)
    if bias is None:
        s = s_ref[live, :]
    else:
        s = jnp.concatenate([s_ref[r0:r0 + bk, :] + bias[r0 // bk] for r0 in range(row_start, bq, bk)], axis=0)
    if mask_off is not None:
        row = lax.broadcasted_iota(jnp.int32, s.shape, 0) + row_start
        col = lax.broadcasted_iota(jnp.int32, s.shape, 1) + mask_off
        s = jnp.where(col <= row, s, NEG_BIG)
    m_prev = m_ref[live, :]
    m_new = jnp.maximum(m_prev, jnp.max(s, axis=-1, keepdims=True))
    alpha = jnp.exp2(m_prev - m_new)
    p = jnp.exp2(s - _lanes(m_new, bk))
    pv = jnp.dot(p.astype(BF16), v_aug, preferred_element_type=F32)
    acc_ref[live, :] = _lanes(alpha, 2 * HEAD_DIM) * acc_ref[live, :] + pv
    m_ref[live, :] = m_new


def _init_stats(m_ref, acc_ref):
    m_ref[...] = jnp.full_like(m_ref, NEG_BIG)
    acc_ref[...] = jnp.zeros_like(acc_ref)


def _normalised(acc_ref):
    acc = acc_ref[...]
    return acc[:, :HEAD_DIM] / acc[:, HEAD_DIM:]


def _fill_v_aug(vaug_ref, v_ref):
    vaug_ref[:, :HEAD_DIM] = v_ref[0]
    vaug_ref[:, HEAD_DIM:] = jnp.ones((vaug_ref.shape[0], HEAD_DIM), BF16)


def _softmax_attn_kernel(q_ref, k_ref, v_ref, g_ref, kx_ref, _, o_ref, kaug_ref, vaug_ref, m_ref, acc_ref,
                         sa_ref, sb_ref, *, fox, bq, bk):
    i = pl.program_id(2)

    @pl.when(i == 0)
    def _():
        kaug_ref[:, :HEAD_DIM] = k_ref[0]
        kaug_ref[:, HEAD_DIM:] = kx_ref[0, 0]
        _fill_v_aug(vaug_ref, v_ref)

    q = q_ref[0]
    if fox:
        lane = lax.broadcasted_iota(jnp.int32, (bq, HEAD_DIM), 1)
        q = jnp.concatenate([q, jnp.where(lane < 3, 1.0, 0.0).astype(BF16)], axis=1)
    _init_stats(m_ref, acc_ref)

    def logits(s_ref, j, row_start=0):
        s_ref[row_start:, :] = _qk(q[row_start:], _rows(kaug_ref, j, bk))

    def update(s_ref, j, **kw):
        _softmax_update(s_ref, _rows(vaug_ref, j, bk), m_ref, acc_ref, **kw)

    logits(sa_ref, 0)

    def pair(j):
        logits(sb_ref, j + 1)
        update(sa_ref, j)
        logits(sa_ref, j + 2)
        update(sb_ref, j + 1)

    _walk_pairs(pair, i)
    logits(sb_ref, 2 * i + 1, row_start=bk)
    update(sa_ref, 2 * i, mask_off=0)
    update(sb_ref, 2 * i + 1, mask_off=bk, row_start=bk)
    o_ref[0] = _rms(_normalised(acc_ref), g_ref[0]).astype(o_ref.dtype)


def _diff_attn_kernel(q_ref, k_ref, v_ref, g_ref, bdiag_ref, bsub_ref, far_ref, lam_ref, _, o_ref,
                      vaug_ref, m1_ref, a1_ref, m2_ref, a2_ref, s1a_ref, s2a_ref, s1b_ref, s2b_ref,
                      *, bq, bk, out_scale):
    i = pl.program_id(2)

    @pl.when(i == 0)
    def _():
        _fill_v_aug(vaug_ref, v_ref)

    q = q_ref[0]
    half = lax.broadcasted_iota(jnp.int32, q.shape, 1) < (HEAD_DIM // 2)
    zero = jnp.zeros_like(q)
    q1 = jnp.where(half, q, zero)
    q2 = jnp.where(half, zero, q)
    _init_stats(m1_ref, a1_ref)
    _init_stats(m2_ref, a2_ref)
    k_ref = k_ref.at[0]
    buf_a = (s1a_ref, s2a_ref)
    buf_b = (s1b_ref, s2b_ref)

    def logits(buf, j, row_start=0):
        k = _rows(k_ref, j, bk)
        buf[0][row_start:, :] = _qk(q1[row_start:], k)
        buf[1][row_start:, :] = _qk(q2[row_start:], k)

    far = _lanes(far_ref[0], bk)

    def update(buf, j, bias_top, bias_bottom, **kw):
        v = _rows(vaug_ref, j, bk)
        bias = None if bias_top is None and bias_bottom is None else (bias_top, bias_bottom)
        for s_ref, m_ref, a_ref in ((buf[0], m1_ref, a1_ref), (buf[1], m2_ref, a2_ref)):
            _softmax_update(s_ref, v, m_ref, a_ref, bias=bias, **kw)

    logits(buf_a, 0)

    def pair(j):
        logits(buf_b, j + 1)
        update(buf_a, j, None, None)
        logits(buf_a, j + 2)
        update(buf_b, j + 1, None, None)

    _walk_pairs(pair, i - 1)
    m1_ref[...] = m1_ref[...] + far_ref[0]
    m2_ref[...] = m2_ref[...] + far_ref[0]

    @pl.when(i >= 1)
    def _():
        logits(buf_b, 2 * i - 1)
        update(buf_a, 2 * i - 2, far, far)
        logits(buf_a, 2 * i)
        update(buf_b, 2 * i - 1, bsub_ref[0], far)

    logits(buf_b, 2 * i + 1, row_start=bk)
    update(buf_a, 2 * i, bdiag_ref[0], bsub_ref[0], mask_off=0)
    update(buf_b, 2 * i + 1, None, bdiag_ref[0], mask_off=bk, row_start=bk)
    o = _normalised(a1_ref) - lam_ref[...] * _normalised(a2_ref)
    o_ref[0] = (_rms(o, g_ref[0]) * out_scale).astype(o_ref.dtype)


def _sb_attn_kernel(q_ref, k_ref, v_ref, g_ref, u_ref, _, o_ref, r_ref, acc_ref, za_ref, zb_ref, *, bq, bk):
    i = pl.program_id(2)
    q = q_ref[0]
    k_ref = k_ref.at[0]
    v_ref = v_ref.at[0]
    r_ref[...] = jnp.zeros_like(r_ref)
    acc_ref[...] = jnp.zeros_like(acc_ref)
    u = u_ref[...]
    n_groups = bk // SB_GROUP

    def logits(z_ref, j):
        z_ref[...] = _qk(q, _rows(k_ref, jnp.maximum(j, 0), bk))

    def step(z_ref, j, mask_off=None):
        z = z_ref[...]
        r = r_ref[...]
        parts = [None] * n_groups
        for g in reversed(range(n_groups)):
            zg = jnp.minimum(z[:, g * SB_GROUP:(g + 1) * SB_GROUP], SB_LOGIT_CAP)
            sp = jnp.log2(1.0 + jnp.exp2(zg))
            if mask_off is not None:
                strict = _causal_mask(bq, SB_GROUP, mask_off + g * SB_GROUP, strict=True)
                sp = jnp.where(strict, sp, 0.0)
            between = jnp.dot(sp.astype(BF16), u, preferred_element_type=F32)
            a = jnp.exp2((zg - sp) + between + _lanes(r, SB_GROUP))
            if mask_off is not None:
                a = jnp.where(strict, a, 0.0)
            parts[g] = a.astype(BF16)
            r = r - jnp.sum(sp, axis=-1, keepdims=True)
        r_ref[...] = r
        acc_ref[...] += jnp.dot(jnp.concatenate(parts, axis=1), _rows(v_ref, j, bk),
                                preferred_element_type=F32)

    logits(za_ref, 2 * i + 1)
    logits(zb_ref, 2 * i)
    step(za_ref, 2 * i + 1, mask_off=bk)
    logits(za_ref, 2 * i - 1)
    step(zb_ref, 2 * i, mask_off=0)

    def pair(jj):
        j = 2 * i - 1 - jj
        logits(zb_ref, j - 1)
        step(za_ref, j)
        logits(za_ref, j - 2)
        step(zb_ref, j - 1)

    _walk_pairs(pair, i)

    o_ref[0] = _rms(acc_ref[...], g_ref[0]).astype(o_ref.dtype)


def _attn_call(kernel, o_buf, head_off, n_heads, q_arr, k_arr, v_arr, gain, extra_in, extra_specs, scratch,
               dq, q_col, k_col, v_col, name):
    nb, seq, _ = q_arr.shape
    bq, bk = _attn_blocks(seq)
    in_specs = [pl.BlockSpec((1, bq, dq), lambda b, h, i: (b, i, q_col(h))),
                pl.BlockSpec((1, seq, HEAD_DIM), lambda b, h, i: (b, 0, k_col(h))),
                pl.BlockSpec((1, seq, HEAD_DIM), lambda b, h, i: (b, 0, v_col(h))),
                pl.BlockSpec((1, 1, HEAD_DIM), lambda b, h, i: (h, 0, 0))]
    in_specs += extra_specs + [pl.BlockSpec(memory_space=pl.ANY)]
    n_in = len(in_specs)
    return pl.pallas_call(
        functools.partial(kernel, bq=bq, bk=bk),
        out_shape=jax.ShapeDtypeStruct(o_buf.shape, o_buf.dtype),
        grid=(nb, n_heads, seq // bq),
        in_specs=in_specs,
        out_specs=pl.BlockSpec((1, bq, HEAD_DIM), lambda b, h, i: (b, i, head_off + h)),
        scratch_shapes=scratch,
        input_output_aliases={n_in - 1: 0},
        compiler_params=_params("arbitrary", "arbitrary", "arbitrary", flags=ATTN_FLAGS), name=name,
    )(q_arr, k_arr, v_arr, gain.reshape(n_heads, 1, HEAD_DIM).astype(F32), *extra_in, o_buf)


def _attn_blocks(seq):
    bk = min(ATTN_KEY_BLOCK, seq // 2)
    assert seq % (2 * bk) == 0 and bk % SB_GROUP == 0
    return 2 * bk, bk


def _stat_scratch(bq, n):
    out = []
    for _ in range(n):
        out += [pltpu.VMEM((bq, HEAD_DIM), F32), pltpu.VMEM((bq, 2 * HEAD_DIM), F32)]
    return out


def _logit_scratch(bq, bk, n):
    return [pltpu.VMEM((bq, bk), F32)] * n


def _group(off):
    return lambda h: off + h


def sb_attention(o_buf, head_off, qkv, gain, q_off, k_off, v_off):
    bq, bk = _attn_blocks(qkv.shape[1])
    u = -jnp.tril(jnp.ones((SB_GROUP, SB_GROUP), F32), -1).astype(BF16)
    return _attn_call(_sb_attn_kernel, o_buf, head_off, N_SB, qkv, qkv, qkv, gain, [u],
                      [pl.BlockSpec((SB_GROUP, SB_GROUP), lambda b, h, i: (0, 0))],
                      [pltpu.VMEM((bq, HEAD_DIM), F32), pltpu.VMEM((bq, HEAD_DIM), F32)]
                      + _logit_scratch(bq, bk, 2),
                      HEAD_DIM, _group(q_off), _group(k_off), _group(v_off), "sb_attention")


def _toeplitz(g, n):
    nh = g.shape[0]
    t = jnp.broadcast_to(g[:, None, :], (nh, n, 2 * n - 1))
    t = jnp.pad(t, ((0, 0), (0, 0), (0, 1))).reshape(nh, 2 * n * n)[:, :n * (2 * n - 1)]
    return t.reshape(nh, n, 2 * n - 1)[:, :, n - 1:]


def diff_attention(o_buf, head_off, qkv, gain, t5_table, lam, out_scale, q_off, k_off, v_off):
    seq = qkv.shape[1]
    bq, bk = _attn_blocks(seq)
    assert bk >= MAX_DISTANCE
    table = t5_table.astype(F32) * LOG2E
    f_table = table[_t5_bucket(jnp.arange(2 * bk, dtype=jnp.int32))].T
    x = jnp.arange(2 * bk - 1)
    b_diag = _toeplitz(f_table[:, jnp.maximum(bk - 1 - x, 0)], bk)
    b_sub = _toeplitz(f_table[:, 2 * bk - 1 - x], bk)
    far = jnp.broadcast_to(table[N_BUCKETS - 1][:, None, None], (N_DIFF, 1, HEAD_DIM))
    lam_row = jnp.broadcast_to(lam.astype(F32).reshape(1, 1), (1, HEAD_DIM))
    blk = pl.BlockSpec((1, bk, bk), lambda b, h, i: (h, 0, 0))
    return _attn_call(functools.partial(_diff_attn_kernel, out_scale=out_scale), o_buf, head_off, N_DIFF,
                      qkv, qkv, qkv, gain, [b_diag, b_sub, far, lam_row],
                      [blk, blk, pl.BlockSpec((1, 1, HEAD_DIM), lambda b, h, i: (h, 0, 0)),
                       pl.BlockSpec((1, HEAD_DIM), lambda b, h, i: (0, 0))],
                      [pltpu.VMEM((seq, 2 * HEAD_DIM), BF16)] + _stat_scratch(bq, 2)
                      + _logit_scratch(bq, bk, 4),
                      HEAD_DIM, _group(q_off), _group(k_off), _group(v_off), "diff_attention")


def _wide_softmax_attention(o_buf, head_off, fox, n_heads, q_arr, dq, q_col, kv_arr, k_col, v_col, kx, kx_spec,
                            gain, name):
    seq = q_arr.shape[1]
    bq, bk = _attn_blocks(seq)
    scratch = [pltpu.VMEM((seq, 2 * HEAD_DIM), BF16), pltpu.VMEM((seq, 2 * HEAD_DIM), BF16)]
    return _attn_call(functools.partial(_softmax_attn_kernel, fox=fox), o_buf, head_off, n_heads,
                      q_arr, kv_arr, kv_arr, gain, [kx], [kx_spec],
                      scratch + _stat_scratch(bq, 1) + _logit_scratch(bq, bk, 2),
                      dq, q_col, k_col, v_col, name)


def mla_attention(o_buf, head_off, q_cat, kv, k_rope_pad, gain):
    nb, seq, _ = q_cat.shape
    spec = pl.BlockSpec((1, 1, seq, HEAD_DIM), lambda b, h, i: (b, 0, 0, 0))
    return _wide_softmax_attention(o_buf, head_off, False, N_MLA, q_cat, MLA_QK_PAD, lambda h: h,
                                   kv, lambda h: 2 * h, lambda h: 2 * h + 1,
                                   k_rope_pad.reshape(nb, 1, seq, HEAD_DIM), spec, gain, "mla_attention")


def fox_attention(o_buf, head_off, qkv, gain, c, q_off, k_off, v_off):
    nb, seq, _ = qkv.shape
    c2 = jnp.moveaxis(c, -1, 1) * LOG2E
    hi = c2.astype(BF16)
    mid = (c2 - hi.astype(F32)).astype(BF16)
    lo = (c2 - hi.astype(F32) - mid.astype(F32)).astype(BF16)
    kx = jnp.stack([-hi, -mid, -lo], axis=-1)
    kx = jnp.pad(kx, ((0, 0), (0, 0), (0, 0), (0, HEAD_DIM - 3)))
    spec = pl.BlockSpec((1, 1, seq, HEAD_DIM), lambda b, h, i: (b, h, 0, 0))
    return _wide_softmax_attention(o_buf, head_off, True, N_FOX, qkv, HEAD_DIM, _group(q_off),
                                   qkv, _group(k_off), _group(v_off), kx, spec, gain, "fox_attention")


def _xattn_kernel(q_ref, k_ref, v_ref, o_ref):
    for h in range(N_XHEADS):
        sl = slice(h * HEAD_DIM, (h + 1) * HEAD_DIM)
        s = _qk(q_ref[0, :, sl], k_ref[0, :, sl])
        p = jnp.exp(s - jnp.max(s, axis=-1, keepdims=True))
        o = jnp.dot(p.astype(BF16), v_ref[0, :, sl], preferred_element_type=F32)
        o_ref[0, :, sl] = (o / jnp.sum(p, axis=-1, keepdims=True)).astype(o_ref.dtype)


def cross_attention(q, k, v, bq=512):
    nb, seq, w = q.shape
    n_mem = k.shape[1]
    bq = min(bq, seq)
    return pl.pallas_call(
        _xattn_kernel,
        out_shape=jax.ShapeDtypeStruct((nb, seq, w), BF16),
        grid=(nb, seq // bq),
        in_specs=[pl.BlockSpec((1, bq, w), lambda b, i: (b, i, 0)),
                  pl.BlockSpec((1, n_mem, w), lambda b, i: (b, 0, 0)),
                  pl.BlockSpec((1, n_mem, w), lambda b, i: (b, 0, 0))],
        out_specs=pl.BlockSpec((1, bq, w), lambda b, i: (b, i, 0)),
        compiler_params=_params("parallel", "parallel"), name="cross_attention",
    )(q, k, v)


def _t5_bucket(rel):
    n = jnp.maximum(rel, 0)
    max_exact = N_BUCKETS // 2
    nf = jnp.maximum(n, 1).astype(F32)
    large = max_exact + (jnp.log(nf / max_exact) / math.log(MAX_DISTANCE / max_exact)
                         * (N_BUCKETS - max_exact)).astype(jnp.int32)
    large = jnp.minimum(large, N_BUCKETS - 1)
    return jnp.where(n < max_exact, n, large)


def _rope(x, cos, sin):
    x1, x2 = jnp.split(x, 2, axis=-1)
    return jnp.concatenate([x1 * cos - x2 * sin, x2 * cos + x1 * sin], axis=-1)


HW = N_SB * HEAD_DIM
IN_DF = 3 * HW
IN_CQ = 6 * HW
IN_CKV = IN_CQ + MLA_Q_RANK
IN_KR = IN_CKV + MLA_KV_RANK
IN_FX = IN_KR + MLA_ROPE
IN_FL = IN_FX + 3 * HW
SMALL_CKV = 2 * MLA_KV_RANK
SMALL_KR = SMALL_CKV + MLA_KV_RANK
SMALL_FL = SMALL_KR + HEAD_DIM
SMALL_N = SMALL_FL + HEAD_DIM


def _prep_w_in_kernel(w_ref, scale_ref, main_ref, small_ref):
    w = w_ref[...] * scale_ref[...]
    main_ref[:, :IN_CQ] = w[:, :IN_CQ].astype(BF16)
    main_ref[:, IN_CQ:] = w[:, IN_FX:IN_FL].astype(BF16)
    small_ref[...] = jnp.zeros_like(small_ref)
    small_ref[:, :MLA_Q_RANK] = w[:, IN_CQ:IN_CKV].astype(BF16)
    small_ref[:, SMALL_CKV:SMALL_KR] = w[:, IN_CKV:IN_KR].astype(BF16)
    small_ref[:, SMALL_KR:SMALL_KR + MLA_ROPE] = w[:, IN_KR:IN_FX].astype(BF16)
    small_ref[:, SMALL_FL:SMALL_FL + N_FOX] = w[:, IN_FL:IN_FL + N_FOX].astype(BF16)


def _prep_w_in(w_in, bd=128):
    nl, d, n_in = w_in.shape
    sc = HEAD_DIM ** -0.5 * LOG2E
    sc_df = (HEAD_DIM // 2) ** -0.5 * LOG2E
    col_scale = jnp.ones((1, n_in), F32)
    col_scale = col_scale.at[:, 0:HW].set(sc).at[:, IN_DF:IN_DF + HW].set(sc_df).at[:, IN_FX:IN_FX + HW].set(sc)
    bd = _pick_block(d, bd)
    return pl.pallas_call(
        _prep_w_in_kernel,
        out_shape=(jax.ShapeDtypeStruct((nl, d, 9 * HW), BF16), jax.ShapeDtypeStruct((nl, d, SMALL_N), BF16)),
        grid=(nl, d // bd),
        in_specs=[pl.BlockSpec((None, bd, n_in), lambda l, i: (l, i, 0)),
                  pl.BlockSpec((1, n_in), lambda l, i: (0, 0))],
        out_specs=(pl.BlockSpec((None, bd, 9 * HW), lambda l, i: (l, i, 0)),
                   pl.BlockSpec((None, bd, SMALL_N), lambda l, i: (l, i, 0))),
        compiler_params=_params("parallel", "parallel"), name="prep_w_in",
    )(w_in, col_scale)


def _mixer(h, nb, seq, cos, sin, t5_table, w_main, w_small, q_norm, w_uq, kv_norm, w_ukv, lam_vecs, b_forget,
           head_norm, w_out, layer_idx):
    sc_mla = (MLA_NOPE + MLA_ROPE) ** -0.5 * LOG2E
    qkv = matmul(h, w_main, BF16, layer=layer_idx).reshape(nb, seq, 9 * HW)
    small = matmul(h, w_small, F32, bm=512, bn=SMALL_N, layer=layer_idx)
    k_r = small[:, SMALL_KR:SMALL_KR + MLA_ROPE]
    f_logit = small[:, SMALL_FL:SMALL_FL + N_FOX]

    g_sb, g_df, g_mla, g_fx = jnp.split(head_norm, [N_SB, N_SB + N_DIFF, N_SB + N_DIFF + N_MLA], axis=0)
    o = jnp.zeros((nb, seq, (N_SB + N_DIFF + N_MLA + N_FOX) * HEAD_DIM), BF16)

    o = sb_attention(o, 0, qkv, g_sb, 0, N_SB, 2 * N_SB)

    lv = lam_vecs.astype(F32)
    lam_init = 0.8 - 0.6 * math.exp(-0.3 * layer_idx)
    lam = jnp.exp(jnp.sum(lv[0] * lv[1])) - jnp.exp(jnp.sum(lv[2] * lv[3])) + lam_init
    o = diff_attention(o, N_SB, qkv, g_df, t5_table, lam, 1.0 - lam_init, 3 * N_SB, 4 * N_SB, 5 * N_SB)

    wq = (w_uq * sc_mla).reshape(MLA_Q_RANK, N_MLA, MLA_NOPE + MLA_ROPE)
    w_r = wq[..., MLA_NOPE:]
    w_rot = jnp.concatenate([w_r[..., MLA_ROPE // 2:], w_r[..., :MLA_ROPE // 2]], axis=-1)
    wq = jnp.concatenate([wq, w_rot], axis=-1).reshape(MLA_Q_RANK, N_MLA * MLA_QK_PAD).astype(BF16)
    zeros = jnp.zeros_like(cos)
    cs = jnp.concatenate([cos, cos, zeros, zeros], axis=-1).reshape(nb * seq, HEAD_DIM)
    sn = jnp.concatenate([-sin, sin, zeros, zeros], axis=-1).reshape(nb * seq, HEAD_DIM)
    q_cat = norm_matmul(small, 0, MLA_Q_RANK, q_norm, wq, rope=(cs, sn, N_MLA))
    q_cat = q_cat.reshape(nb, seq, N_MLA * MLA_QK_PAD)
    kv = norm_matmul(small, SMALL_CKV // MLA_KV_RANK, MLA_KV_RANK, kv_norm, w_ukv.astype(BF16))
    kv = kv.reshape(nb, seq, N_MLA * 2 * HEAD_DIM)
    k_rope = _rope(k_r.reshape(nb, seq, MLA_ROPE), cos, sin)
    k_rope_pad = jnp.pad(k_rope, ((0, 0), (0, 0), (0, HEAD_DIM - MLA_ROPE))).astype(BF16)
    o = mla_attention(o, N_SB + N_DIFF, q_cat, kv, k_rope_pad, g_mla)

    log_f = jax.nn.log_sigmoid(f_logit.reshape(nb, seq, N_FOX) + b_forget.astype(F32))
    c = lax.cumsum(log_f, axis=1)
    o = fox_attention(o, N_SB + N_DIFF + N_MLA, qkv, g_fx, c, 6 * N_SB, 7 * N_SB, 8 * N_SB)

    return matmul(o.reshape(nb * seq, -1), w_out, F32, layer=layer_idx)


def kernel(x, mem, positions, t5_table, mix_norm_pre, w_in, mla_q_norm, w_uq, mla_kv_norm, w_ukv, diff_lambda, b_forget, head_norm, w_out, mix_norm_post, xattn_norm_pre, mem_norm, w_xq, w_xk, w_xv, w_xo, xattn_norm_post, ffn_norm_pre, w_gate, w_up, conv_w, conv_b, w_down, ffn_norm_post):
    nb, seq, d = x.shape
    depth = w_in.shape[0]
    n_mem = mem.shape[1]
    d_ff = w_gate.shape[2]
    xw = N_XHEADS * HEAD_DIM
    inv = ROPE_THETA ** (-jnp.arange(0, MLA_ROPE, 2, dtype=F32) / MLA_ROPE)
    ang = positions.astype(F32)[..., None] * inv
    cos, sin = jnp.cos(ang), jnp.sin(ang)

    xf = x.reshape(nb * seq, d)
    memf = mem.reshape(nb * n_mem, d)
    w_main, w_small = _prep_w_in(w_in)
    w_out, w_xo = w_out.astype(BF16), w_xo.astype(BF16)
    w_gate, w_up, w_down = w_gate.astype(BF16), w_up.astype(BF16), w_down.astype(BF16)
    h = rmsnorm(xf, mix_norm_pre[0])
    for l in range(depth):
        y = _mixer(h, nb, seq, cos, sin, t5_table, w_main, w_small, mla_q_norm[l], w_uq[l], mla_kv_norm[l],
                   w_ukv[l], diff_lambda[l], b_forget[l], head_norm[l], w_out, l)
        xf, h = addnorm(y, xf, mix_norm_post[l], xattn_norm_pre[l])

        mem_n = rmsnorm(memf, mem_norm[l])
        q = matmul(h, (w_xq[l] * HEAD_DIM ** -0.5).astype(BF16), BF16).reshape(nb, seq, xw)
        k = matmul(mem_n, w_xk[l].astype(BF16), BF16).reshape(nb, n_mem, xw)
        v = matmul(mem_n, w_xv[l].astype(BF16), BF16).reshape(nb, n_mem, xw)
        o = cross_attention(q, k, v).reshape(nb * seq, xw)
        xf, h = matmul_addnorm(o, w_xo, xf, xattn_norm_post[l], ffn_norm_pre[l], layer=l)

        act_main, act_tail = ffn_up(h, w_gate, w_up, conv_w[l], conv_b[l], seq, layer=l)
        y = ffn_down(act_main, act_tail, w_down, l)
        g_next = mix_norm_pre[l + 1] if l + 1 < depth else None
        xf, h = addnorm(y, xf, ffn_norm_post[l], g_next)
    return xf.reshape(nb, seq, d)
```

```python
import functools
import math

import jax
import jax.numpy as jnp
from jax import lax
from jax.experimental import pallas as pl
from jax.experimental.pallas import tpu as pltpu

F32 = jnp.float32
BF16 = jnp.bfloat16

HEAD_DIM = 128
N_SB = 8
N_DIFF = 8
N_MLA = 8
N_FOX = 8
MLA_Q_RANK = 768
MLA_KV_RANK = 512
MLA_NOPE = 128
MLA_ROPE = 64
MLA_QK_PAD = 256
N_XHEADS = 4
N_BUCKETS = 32
MAX_DISTANCE = 128
ROPE_THETA = 10000.0
EPS = 1e-6
NEG_BIG = -1e30
LOG2E = math.log2(math.e)

V7X_VMEM_LIMIT_BYTES = 56 * 1024 * 1024
ATTN_KEY_BLOCK = 512
SB_GROUP = 256
SB_LOGIT_CAP = 120.0
FFN_CHUNK = 256


def _params(*sem, flags=None):
    return pltpu.CompilerParams(dimension_semantics=sem, vmem_limit_bytes=V7X_VMEM_LIMIT_BYTES, flags=flags)


ATTN_FLAGS = None


def _rms(x, g):
    return x * lax.rsqrt(jnp.mean(x * x, axis=-1, keepdims=True) + EPS) * g


def _rmsnorm_kernel(x_ref, g_ref, o_ref):
    o_ref[...] = _rms(x_ref[...].astype(F32), g_ref[...]).astype(o_ref.dtype)


def rmsnorm(x, g, out_dtype=BF16, bm=512):
    m, d = x.shape
    bm = min(bm, m)
    return pl.pallas_call(
        _rmsnorm_kernel,
        out_shape=jax.ShapeDtypeStruct((m, d), out_dtype),
        grid=(pl.cdiv(m, bm),),
        in_specs=[pl.BlockSpec((bm, d), lambda i: (i, 0)), pl.BlockSpec((1, d), lambda i: (0, 0))],
        out_specs=pl.BlockSpec((bm, d), lambda i: (i, 0)),
        compiler_params=_params("parallel"),
        name="rmsnorm",
    )(x, g.reshape(1, d).astype(F32))


def _addnorm_kernel(y_ref, x_ref, gpost_ref, gnext_ref, xo_ref, ho_ref):
    xn = x_ref[...] + _rms(y_ref[...], gpost_ref[...])
    xo_ref[...] = xn
    ho_ref[...] = _rms(xn, gnext_ref[...]).astype(ho_ref.dtype)


def _addnorm_last_kernel(y_ref, x_ref, gpost_ref, xo_ref):
    xo_ref[...] = x_ref[...] + _rms(y_ref[...], gpost_ref[...])


def addnorm(y, x, g_post, g_next=None, bm=256):
    m, d = x.shape
    bm = min(bm, m)
    row = pl.BlockSpec((bm, d), lambda i: (i, 0))
    vec = pl.BlockSpec((1, d), lambda i: (0, 0))
    if g_next is None:
        return pl.pallas_call(
            _addnorm_last_kernel,
            out_shape=jax.ShapeDtypeStruct((m, d), F32),
            grid=(pl.cdiv(m, bm),), in_specs=[row, row, vec], out_specs=row,
            compiler_params=_params("parallel"), name="addnorm_last",
        )(y, x, g_post.reshape(1, d)), None
    return pl.pallas_call(
        _addnorm_kernel,
        out_shape=(jax.ShapeDtypeStruct((m, d), F32), jax.ShapeDtypeStruct((m, d), BF16)),
        grid=(pl.cdiv(m, bm),), in_specs=[row, row, vec, vec], out_specs=(row, row),
        compiler_params=_params("parallel"), name="addnorm",
    )(y, x, g_post.reshape(1, d), g_next.reshape(1, d))


def _matmul_kernel(a_ref, b_ref, o_ref):
    o_ref[...] = jnp.dot(a_ref[...], b_ref[...], preferred_element_type=F32).astype(o_ref.dtype)


def _matmul_acc_kernel(a_ref, b_ref, o_ref, acc_ref, *, nk):
    k = pl.program_id(2)

    @pl.when(k == 0)
    def _():
        acc_ref[...] = jnp.zeros_like(acc_ref)

    acc_ref[...] += jnp.dot(a_ref[...], b_ref[...], preferred_element_type=F32)

    @pl.when(k == nk - 1)
    def _():
        o_ref[...] = acc_ref[...].astype(o_ref.dtype)


def _pick_block(n, target):
    if n <= target:
        return n
    best = 128
    for cand in range(128, target + 1, 128):
        if n % cand == 0:
            best = cand
    assert n % best == 0
    return best


def _wspec(w, layer, block, index_map):
    if w.ndim == 2:
        return pl.BlockSpec(block, index_map)
    return pl.BlockSpec((None,) + tuple(block), lambda *g: (layer,) + tuple(index_map(*g)))


def _matmul_t_kernel(a_ref, b_ref, o_ref):
    o_ref[...] = lax.dot_general(a_ref[...], b_ref[...], (((1,), (1,)), ((), ())),
                                 preferred_element_type=F32).astype(o_ref.dtype)


def matmul(a, b, out_dtype, bm=1024, bn=1024, bk=None, layer=None, rhs_t=False):
    m, kdim = a.shape
    if rhs_t:
        n = b.shape[-2]
        bm, bn = _pick_block(m, bm), _pick_block(n, bn)
        return pl.pallas_call(
            _matmul_t_kernel,
            out_shape=jax.ShapeDtypeStruct((m, n), out_dtype),
            grid=(m // bm, n // bn),
            in_specs=[pl.BlockSpec((bm, kdim), lambda i, j: (i, 0)),
                      _wspec(b, layer, (bn, kdim), lambda i, j: (j, 0))],
            out_specs=pl.BlockSpec((bm, bn), lambda i, j: (i, j)),
            compiler_params=_params("parallel", "parallel"), name="matmul_t",
        )(a, b)
    n = b.shape[-1]
    bm, bn = _pick_block(m, bm), _pick_block(n, bn)
    bk = kdim if bk is None else bk
    assert kdim % bk == 0
    nk = kdim // bk
    if nk == 1:
        return pl.pallas_call(
            _matmul_kernel,
            out_shape=jax.ShapeDtypeStruct((m, n), out_dtype),
            grid=(pl.cdiv(m, bm), pl.cdiv(n, bn)),
            in_specs=[pl.BlockSpec((bm, kdim), lambda i, j: (i, 0)),
                      _wspec(b, layer, (kdim, bn), lambda i, j: (0, j))],
            out_specs=pl.BlockSpec((bm, bn), lambda i, j: (i, j)),
            compiler_params=_params("parallel", "parallel"), name="matmul",
        )(a, b)
    return pl.pallas_call(
        functools.partial(_matmul_acc_kernel, nk=nk),
        out_shape=jax.ShapeDtypeStruct((m, n), out_dtype),
        grid=(pl.cdiv(m, bm), pl.cdiv(n, bn), nk),
        in_specs=[pl.BlockSpec((bm, bk), lambda i, j, k: (i, k)),
                  _wspec(b, layer, (bk, bn), lambda i, j, k: (k, j))],
        out_specs=pl.BlockSpec((bm, bn), lambda i, j, k: (i, j)),
        scratch_shapes=[pltpu.VMEM((bm, bn), F32)],
        compiler_params=_params("parallel", "parallel", "arbitrary"), name="matmul_acc",
    )(a, b)


def _matmul_addnorm_kernel(a_ref, w_ref, x_ref, gpost_ref, gnext_ref, xo_ref, ho_ref):
    y = jnp.dot(a_ref[...], w_ref[...], preferred_element_type=F32)
    xn = x_ref[...] + _rms(y, gpost_ref[...])
    xo_ref[...] = xn
    ho_ref[...] = _rms(xn, gnext_ref[...]).astype(ho_ref.dtype)


def matmul_addnorm(a, w, x, g_post, g_next, bm=256, layer=None):
    m, kdim = a.shape
    d = w.shape[-1]
    bm = _pick_block(m, bm)
    row = pl.BlockSpec((bm, d), lambda i: (i, 0))
    vec = pl.BlockSpec((1, d), lambda i: (0, 0))
    return pl.pallas_call(
        _matmul_addnorm_kernel,
        out_shape=(jax.ShapeDtypeStruct((m, d), F32), jax.ShapeDtypeStruct((m, d), BF16)),
        grid=(m // bm,),
        in_specs=[pl.BlockSpec((bm, kdim), lambda i: (i, 0)), _wspec(w, layer, (kdim, d), lambda i: (0, 0)),
                  row, vec, vec],
        out_specs=(row, row),
        compiler_params=_params("parallel"), name="matmul_addnorm",
    )(a, w, x, g_post.reshape(1, d), g_next.reshape(1, d))


def _norm_matmul_kernel(x_ref, g_ref, w_ref, *rest, rope_heads):
    o_ref = rest[-1]
    xn = _rms(x_ref[...], g_ref[...]).astype(BF16)
    y = jnp.dot(xn, w_ref[...], preferred_element_type=F32)
    if not rope_heads:
        o_ref[...] = y.astype(o_ref.dtype)
        return
    cs, sn = rest[0][...], rest[1][...]
    for hd in range(rope_heads):
        c0 = hd * MLA_QK_PAD
        x = y[:, c0 + MLA_NOPE:c0 + MLA_QK_PAD]
        o_ref[:, c0:c0 + MLA_NOPE] = y[:, c0:c0 + MLA_NOPE].astype(o_ref.dtype)
        o_ref[:, c0 + MLA_NOPE:c0 + MLA_QK_PAD] = (x * cs + pltpu.roll(x, MLA_ROPE, axis=1) * sn).astype(o_ref.dtype)


def norm_matmul(x_arr, col_block, kdim, g, w, rope=None, bm=512, layer=None):
    m = x_arr.shape[0]
    n = w.shape[-1]
    bm = _pick_block(m, bm)
    in_specs = [pl.BlockSpec((bm, kdim), lambda i: (i, col_block)),
                pl.BlockSpec((1, kdim), lambda i: (0, 0)),
                _wspec(w, layer, (kdim, n), lambda i: (0, 0))]
    args = [x_arr, g.reshape(1, kdim).astype(F32), w]
    if rope is not None:
        in_specs += [pl.BlockSpec((bm, HEAD_DIM), lambda i: (i, 0))] * 2
        args += [rope[0], rope[1]]
    return pl.pallas_call(
        functools.partial(_norm_matmul_kernel, rope_heads=0 if rope is None else rope[2]),
        out_shape=jax.ShapeDtypeStruct((m, n), BF16),
        grid=(m // bm,), in_specs=in_specs,
        out_specs=pl.BlockSpec((bm, n), lambda i: (i, 0)),
        compiler_params=_params("parallel"), name="norm_matmul",
    )(*args)


def _ffn_up_kernel(h_ref, wg_ref, wu_ref, cw_ref, cb_ref, o_ref, carry_ref, *, tiles_per_seq):
    i = pl.program_id(1)
    bm = h_ref.shape[0]
    bn = o_ref.shape[1]

    @pl.when(i % tiles_per_seq == 0)
    def _():
        carry_ref[...] = jnp.zeros_like(carry_ref)

    row8 = lax.broadcasted_iota(jnp.int32, (8, 1), 0)
    for c0 in range(0, bn, FFN_CHUNK):
        cols = slice(c0, c0 + FFN_CHUNK)
        gate = jnp.dot(h_ref[...], wg_ref[:, cols], preferred_element_type=F32)
        up = jnp.dot(h_ref[...], wu_ref[:, cols], preferred_element_type=F32)
        prev = carry_ref[:, cols]
        carry_ref[:, cols] = gate[bm - 8:, :]
        g1 = pltpu.roll(gate, 1, axis=0)
        g2 = pltpu.roll(gate, 2, axis=0)
        p1 = pltpu.roll(prev, 1, axis=0)
        p2 = pltpu.roll(prev, 2, axis=0)
        g1 = jnp.concatenate([jnp.where(row8 < 1, p1, g1[:8]), g1[8:]], axis=0)
        g2 = jnp.concatenate([jnp.where(row8 < 2, p2, g2[:8]), g2[8:]], axis=0)
        cw = cw_ref[:, cols]
        conv = cw[0:1] * g2 + cw[1:2] * g1 + cw[2:3] * gate + cb_ref[:, cols]
        o_ref[:, cols] = (jax.nn.gelu(conv, approximate=True) * up).astype(o_ref.dtype)


def _ffn_up_call(h, wg, wu, conv_w, conv_b, seq, col0, n_cols, bm, bn, layer):
    m, d = h.shape
    assert n_cols % bn == 0 and col0 % bn == 0 and bn % FFN_CHUNK == 0
    jb = col0 // bn
    return pl.pallas_call(
        functools.partial(_ffn_up_kernel, tiles_per_seq=seq // bm),
        out_shape=jax.ShapeDtypeStruct((m, n_cols), BF16),
        grid=(n_cols // bn, m // bm),
        in_specs=[pl.BlockSpec((bm, d), lambda j, i: (i, 0)),
                  _wspec(wg, layer, (d, bn), lambda j, i: (0, jb + j)),
                  _wspec(wu, layer, (d, bn), lambda j, i: (0, jb + j)),
                  pl.BlockSpec((3, bn), lambda j, i: (0, jb + j)),
                  pl.BlockSpec((1, bn), lambda j, i: (0, jb + j))],
        out_specs=pl.BlockSpec((bm, bn), lambda j, i: (i, j)),
        scratch_shapes=[pltpu.VMEM((8, bn), F32)],
        compiler_params=_params("arbitrary", "arbitrary"), name="ffn_up",
    )(h, wg, wu, conv_w, conv_b)


def ffn_up(h, wg, wu, conv_w, conv_b, seq, bm=1024, bn=512, layer=None):
    m, d = h.shape
    f = wg.shape[-1]
    bm = _pick_block(seq, bm)
    assert seq % bm == 0 and m % seq == 0 and bm % 8 == 0 and f % FFN_CHUNK == 0
    conv_w = conv_w.astype(F32)
    conv_b = conv_b.reshape(1, f).astype(F32)
    bn = min(bn, f)
    n_main = (f // bn) * bn
    main = _ffn_up_call(h, wg, wu, conv_w, conv_b, seq, 0, n_main, bm, bn, layer)
    tail = None
    if n_main < f:
        bn_tail = math.gcd(n_main, f - n_main)
        tail = _ffn_up_call(h, wg, wu, conv_w, conv_b, seq, n_main, f - n_main, bm, bn_tail, layer)
    return main, tail


def _matmul2_kernel(a1_ref, a2_ref, w1_ref, w2_ref, o_ref):
    o_ref[...] = (jnp.dot(a1_ref[...], w1_ref[...], preferred_element_type=F32)
                  + jnp.dot(a2_ref[...], w2_ref[...], preferred_element_type=F32)).astype(o_ref.dtype)


def ffn_down(main, tail, w, layer, bm=512, bn=512):
    if tail is None:
        return matmul(main, w, F32, bm=bm, bn=bn, layer=layer)
    m, k1 = main.shape
    k2 = tail.shape[1]
    n = w.shape[-1]
    assert k1 % k2 == 0 and w.shape[-2] == k1 + k2
    bm, bn = _pick_block(m, bm), _pick_block(n, bn)
    return pl.pallas_call(
        _matmul2_kernel,
        out_shape=jax.ShapeDtypeStruct((m, n), F32),
        grid=(m // bm, n // bn),
        in_specs=[pl.BlockSpec((bm, k1), lambda i, j: (i, 0)),
                  pl.BlockSpec((bm, k2), lambda i, j: (i, 0)),
                  _wspec(w, layer, (k1, bn), lambda i, j: (0, j)),
                  _wspec(w, layer, (k2, bn), lambda i, j: (k1 // k2, j))],
        out_specs=pl.BlockSpec((bm, bn), lambda i, j: (i, j)),
        compiler_params=_params("parallel", "parallel"), name="ffn_down",
    )(main, tail, w, w)


def _qk(q, k):
    return lax.dot_general(q, k, (((1,), (1,)), ((), ())), preferred_element_type=F32)


def _lanes(x, n):
    return x if n == HEAD_DIM else jnp.concatenate([x] * (n // HEAD_DIM), axis=1)


def _rows(ref, j, bk):
    return ref[pl.ds(pl.multiple_of(j * bk, bk), bk), :]


def _causal_mask(bq, bk, col_off, strict=False):
    row = lax.broadcasted_iota(jnp.int32, (bq, bk), 0)
    col = lax.broadcasted_iota(jnp.int32, (bq, bk), 1) + col_off
    return col < row if strict else col <= row


def _walk_pairs(pair, n_pairs):
    def quad(t, c):
        pair(4 * t)
        pair(4 * t + 2)
        return c

    n_pairs = jnp.maximum(n_pairs, 0)
    lax.fori_loop(0, n_pairs // 2, quad, 0)

    @pl.when(n_pairs % 2 == 1)
    def _():
        pair(2 * (n_pairs - 1))


def _softmax_update(s_ref, v_aug, m_ref, acc_ref, bias=None, mask_off=None, row_start=0):
    bq, bk = s_ref.shape
    live = slice(row_start, bq)
    if bias is None:
        s = s_ref[live, :]
    else:
        s = jnp.concatenate([s_ref[r0:r0 + bk, :] + bias[r0 // bk] for r0 in range(row_start, bq, bk)], axis=0)
    if mask_off is not None:
        row = lax.broadcasted_iota(jnp.int32, s.shape, 0) + row_start
        col = lax.broadcasted_iota(jnp.int32, s.shape, 1) + mask_off
        s = jnp.where(col <= row, s, NEG_BIG)
    m_prev = m_ref[live, :]
    m_new = jnp.maximum(m_prev, jnp.max(s, axis=-1, keepdims=True))
    alpha = jnp.exp2(m_prev - m_new)
    p = jnp.exp2(s - _lanes(m_new, bk))
    pv = jnp.dot(p.astype(BF16), v_aug, preferred_element_type=F32)
    acc_ref[live, :] = _lanes(alpha, 2 * HEAD_DIM) * acc_ref[live, :] + pv
    m_ref[live, :] = m_new


def _init_stats(m_ref, acc_ref):
    m_ref[...] = jnp.full_like(m_ref, NEG_BIG)
    acc_ref[...] = jnp.zeros_like(acc_ref)


def _normalised(acc_ref):
    acc = acc_ref[...]
    return acc[:, :HEAD_DIM] / acc[:, HEAD_DIM:]


def _fill_v_aug(vaug_ref, v_ref):
    vaug_ref[:, :HEAD_DIM] = v_ref[0]
    vaug_ref[:, HEAD_DIM:] = jnp.ones((vaug_ref.shape[0], HEAD_DIM), BF16)


def _softmax_attn_kernel(q_ref, k_ref, v_ref, g_ref, kx_ref, _, o_ref, kaug_ref, vaug_ref, m_ref, acc_ref,
                         sa_ref, sb_ref, *, fox, bq, bk):
    i = pl.program_id(2)

    @pl.when(i == 0)
    def _():
        kaug_ref[:, :HEAD_DIM] = k_ref[0]
        kaug_ref[:, HEAD_DIM:] = kx_ref[0, 0]
        _fill_v_aug(vaug_ref, v_ref)

    q = q_ref[0]
    if fox:
        lane = lax.broadcasted_iota(jnp.int32, (bq, HEAD_DIM), 1)
        q = jnp.concatenate([q, jnp.where(lane < 3, 1.0, 0.0).astype(BF16)], axis=1)
    _init_stats(m_ref, acc_ref)

    def logits(s_ref, j, row_start=0):
        s_ref[row_start:, :] = _qk(q[row_start:], _rows(kaug_ref, j, bk))

    def update(s_ref, j, **kw):
        _softmax_update(s_ref, _rows(vaug_ref, j, bk), m_ref, acc_ref, **kw)

    logits(sa_ref, 0)

    def pair(j):
        logits(sb_ref, j + 1)
        update(sa_ref, j)
        logits(sa_ref, j + 2)
        update(sb_ref, j + 1)

    _walk_pairs(pair, i)
    logits(sb_ref, 2 * i + 1, row_start=bk)
    update(sa_ref, 2 * i, mask_off=0)
    update(sb_ref, 2 * i + 1, mask_off=bk, row_start=bk)
    o_ref[0] = _rms(_normalised(acc_ref), g_ref[0]).astype(o_ref.dtype)


def _diff_attn_kernel(q_ref, k_ref, v_ref, g_ref, bdiag_ref, bsub_ref, far_ref, lam_ref, _, o_ref,
                      vaug_ref, m1_ref, a1_ref, m2_ref, a2_ref, s1a_ref, s2a_ref, s1b_ref, s2b_ref,
                      *, bq, bk, out_scale):
    i = pl.program_id(2)

    @pl.when(i == 0)
    def _():
        _fill_v_aug(vaug_ref, v_ref)

    q = q_ref[0]
    half = lax.broadcasted_iota(jnp.int32, q.shape, 1) < (HEAD_DIM // 2)
    zero = jnp.zeros_like(q)
    q1 = jnp.where(half, q, zero)
    q2 = jnp.where(half, zero, q)
    _init_stats(m1_ref, a1_ref)
    _init_stats(m2_ref, a2_ref)
    k_ref = k_ref.at[0]
    buf_a = (s1a_ref, s2a_ref)
    buf_b = (s1b_ref, s2b_ref)

    def logits(buf, j, row_start=0):
        k = _rows(k_ref, j, bk)
        buf[0][row_start:, :] = _qk(q1[row_start:], k)
        buf[1][row_start:, :] = _qk(q2[row_start:], k)

    far = _lanes(far_ref[0], bk)

    def update(buf, j, bias_top, bias_bottom, **kw):
        v = _rows(vaug_ref, j, bk)
        bias = None if bias_top is None and bias_bottom is None else (bias_top, bias_bottom)
        for s_ref, m_ref, a_ref in ((buf[0], m1_ref, a1_ref), (buf[1], m2_ref, a2_ref)):
            _softmax_update(s_ref, v, m_ref, a_ref, bias=bias, **kw)

    logits(buf_a, 0)

    def pair(j):
        logits(buf_b, j + 1)
        update(buf_a, j, None, None)
        logits(buf_a, j + 2)
        update(buf_b, j + 1, None, None)

    _walk_pairs(pair, i - 1)
    m1_ref[...] = m1_ref[...] + far_ref[0]
    m2_ref[...] = m2_ref[...] + far_ref[0]

    @pl.when(i >= 1)
    def _():
        logits(buf_b, 2 * i - 1)
        update(buf_a, 2 * i - 2, far, far)
        logits(buf_a, 2 * i)
        update(buf_b, 2 * i - 1, bsub_ref[0], far)

    logits(buf_b, 2 * i + 1, row_start=bk)
    update(buf_a, 2 * i, bdiag_ref[0], bsub_ref[0], mask_off=0)
    update(buf_b, 2 * i + 1, None, bdiag_ref[0], mask_off=bk, row_start=bk)
    o = _normalised(a1_ref) - lam_ref[...] * _normalised(a2_ref)
    o_ref[0] = (_rms(o, g_ref[0]) * out_scale).astype(o_ref.dtype)


def _sb_attn_kernel(q_ref, k_ref, v_ref, g_ref, u_ref, _, o_ref, r_ref, acc_ref, za_ref, zb_ref, *, bq, bk):
    i = pl.program_id(2)
    q = q_ref[0]
    k_ref = k_ref.at[0]
    v_ref = v_ref.at[0]
    r_ref[...] = jnp.zeros_like(r_ref)
    acc_ref[...] = jnp.zeros_like(acc_ref)
    u = u_ref[...]
    n_groups = bk // SB_GROUP

    def logits(z_ref, j):
        z_ref[...] = _qk(q, _rows(k_ref, jnp.maximum(j, 0), bk))

    def step(z_ref, j, mask_off=None):
        z = z_ref[...]
        r = r_ref[...]
        parts = [None] * n_groups
        for g in reversed(range(n_groups)):
            zg = jnp.minimum(z[:, g * SB_GROUP:(g + 1) * SB_GROUP], SB_LOGIT_CAP)
            sp = jnp.log2(1.0 + jnp.exp2(zg))
            if mask_off is not None:
                strict = _causal_mask(bq, SB_GROUP, mask_off + g * SB_GROUP, strict=True)
                sp = jnp.where(strict, sp, 0.0)
            between = jnp.dot(sp.astype(BF16), u, preferred_element_type=F32)
            a = jnp.exp2((zg - sp) + between + _lanes(r, SB_GROUP))
            if mask_off is not None:
                a = jnp.where(strict, a, 0.0)
            parts[g] = a.astype(BF16)
            r = r - jnp.sum(sp, axis=-1, keepdims=True)
        r_ref[...] = r
        acc_ref[...] += jnp.dot(jnp.concatenate(parts, axis=1), _rows(v_ref, j, bk),
                                preferred_element_type=F32)

    logits(za_ref, 2 * i + 1)
    logits(zb_ref, 2 * i)
    step(za_ref, 2 * i + 1, mask_off=bk)
    logits(za_ref, 2 * i - 1)
    step(zb_ref, 2 * i, mask_off=0)

    def pair(jj):
        j = 2 * i - 1 - jj
        logits(zb_ref, j - 1)
        step(za_ref, j)
        logits(za_ref, j - 2)
        step(zb_ref, j - 1)

    _walk_pairs(pair, i)

    o_ref[0] = _rms(acc_ref[...], g_ref[0]).astype(o_ref.dtype)


def _attn_call(kernel, o_buf, head_off, n_heads, q_arr, k_arr, v_arr, gain, extra_in, extra_specs, scratch,
               dq, q_col, k_col, v_col, name):
    nb, seq, _ = q_arr.shape
    bq, bk = _attn_blocks(seq)
    in_specs = [pl.BlockSpec((1, bq, dq), lambda b, h, i: (b, i, q_col(h))),
                pl.BlockSpec((1, seq, HEAD_DIM), lambda b, h, i: (b, 0, k_col(h))),
                pl.BlockSpec((1, seq, HEAD_DIM), lambda b, h, i: (b, 0, v_col(h))),
                pl.BlockSpec((1, 1, HEAD_DIM), lambda b, h, i: (h, 0, 0))]
    in_specs += extra_specs + [pl.BlockSpec(memory_space=pl.ANY)]
    n_in = len(in_specs)
    return pl.pallas_call(
        functools.partial(kernel, bq=bq, bk=bk),
        out_shape=jax.ShapeDtypeStruct(o_buf.shape, o_buf.dtype),
        grid=(nb, n_heads, seq // bq),
        in_specs=in_specs,
        out_specs=pl.BlockSpec((1, bq, HEAD_DIM), lambda b, h, i: (b, i, head_off + h)),
        scratch_shapes=scratch,
        input_output_aliases={n_in - 1: 0},
        compiler_params=_params("arbitrary", "arbitrary", "arbitrary", flags=ATTN_FLAGS), name=name,
    )(q_arr, k_arr, v_arr, gain.reshape(n_heads, 1, HEAD_DIM).astype(F32), *extra_in, o_buf)


def _attn_blocks(seq):
    bk = min(ATTN_KEY_BLOCK, seq // 2)
    assert seq % (2 * bk) == 0 and bk % SB_GROUP == 0
    return 2 * bk, bk


def _stat_scratch(bq, n):
    out = []
    for _ in range(n):
        out += [pltpu.VMEM((bq, HEAD_DIM), F32), pltpu.VMEM((bq, 2 * HEAD_DIM), F32)]
    return out


def _logit_scratch(bq, bk, n):
    return [pltpu.VMEM((bq, bk), F32)] * n


def _group(off):
    return lambda h: off + h


def sb_attention(o_buf, head_off, qkv, gain, q_off, k_off, v_off):
    bq, bk = _attn_blocks(qkv.shape[1])
    u = -jnp.tril(jnp.ones((SB_GROUP, SB_GROUP), F32), -1).astype(BF16)
    return _attn_call(_sb_attn_kernel, o_buf, head_off, N_SB, qkv, qkv, qkv, gain, [u],
                      [pl.BlockSpec((SB_GROUP, SB_GROUP), lambda b, h, i: (0, 0))],
                      [pltpu.VMEM((bq, HEAD_DIM), F32), pltpu.VMEM((bq, HEAD_DIM), F32)]
                      + _logit_scratch(bq, bk, 2),
                      HEAD_DIM, _group(q_off), _group(k_off), _group(v_off), "sb_attention")


def _toeplitz(g, n):
    nh = g.shape[0]
    t = jnp.broadcast_to(g[:, None, :], (nh, n, 2 * n - 1))
    t = jnp.pad(t, ((0, 0), (0, 0), (0, 1))).reshape(nh, 2 * n * n)[:, :n * (2 * n - 1)]
    return t.reshape(nh, n, 2 * n - 1)[:, :, n - 1:]


def diff_attention(o_buf, head_off, qkv, gain, t5_table, lam, out_scale, q_off, k_off, v_off):
    seq = qkv.shape[1]
    bq, bk = _attn_blocks(seq)
    assert bk >= MAX_DISTANCE
    table = t5_table.astype(F32) * LOG2E
    f_table = table[_t5_bucket(jnp.arange(2 * bk, dtype=jnp.int32))].T
    x = jnp.arange(2 * bk - 1)
    b_diag = _toeplitz(f_table[:, jnp.maximum(bk - 1 - x, 0)], bk)
    b_sub = _toeplitz(f_table[:, 2 * bk - 1 - x], bk)
    far = jnp.broadcast_to(table[N_BUCKETS - 1][:, None, None], (N_DIFF, 1, HEAD_DIM))
    lam_row = jnp.broadcast_to(lam.astype(F32).reshape(1, 1), (1, HEAD_DIM))
    blk = pl.BlockSpec((1, bk, bk), lambda b, h, i: (h, 0, 0))
    return _attn_call(functools.partial(_diff_attn_kernel, out_scale=out_scale), o_buf, head_off, N_DIFF,
                      qkv, qkv, qkv, gain, [b_diag, b_sub, far, lam_row],
                      [blk, blk, pl.BlockSpec((1, 1, HEAD_DIM), lambda b, h, i: (h, 0, 0)),
                       pl.BlockSpec((1, HEAD_DIM), lambda b, h, i: (0, 0))],
                      [pltpu.VMEM((seq, 2 * HEAD_DIM), BF16)] + _stat_scratch(bq, 2)
                      + _logit_scratch(bq, bk, 4),
                      HEAD_DIM, _group(q_off), _group(k_off), _group(v_off), "diff_attention")


def _wide_softmax_attention(o_buf, head_off, fox, n_heads, q_arr, dq, q_col, kv_arr, k_col, v_col, kx, kx_spec,
                            gain, name):
    seq = q_arr.shape[1]
    bq, bk = _attn_blocks(seq)
    scratch = [pltpu.VMEM((seq, 2 * HEAD_DIM), BF16), pltpu.VMEM((seq, 2 * HEAD_DIM), BF16)]
    return _attn_call(functools.partial(_softmax_attn_kernel, fox=fox), o_buf, head_off, n_heads,
                      q_arr, kv_arr, kv_arr, gain, [kx], [kx_spec],
                      scratch + _stat_scratch(bq, 1) + _logit_scratch(bq, bk, 2),
                      dq, q_col, k_col, v_col, name)


def mla_attention(o_buf, head_off, q_cat, kv, k_rope_pad, gain):
    nb, seq, _ = q_cat.shape
    spec = pl.BlockSpec((1, 1, seq, HEAD_DIM), lambda b, h, i: (b, 0, 0, 0))
    return _wide_softmax_attention(o_buf, head_off, False, N_MLA, q_cat, MLA_QK_PAD, lambda h: h,
                                   kv, lambda h: 2 * h, lambda h: 2 * h + 1,
                                   k_rope_pad.reshape(nb, 1, seq, HEAD_DIM), spec, gain, "mla_attention")


def fox_attention(o_buf, head_off, qkv, gain, c, q_off, k_off, v_off):
    nb, seq, _ = qkv.shape
    c2 = jnp.moveaxis(c, -1, 1) * LOG2E
    hi = c2.astype(BF16)
    mid = (c2 - hi.astype(F32)).astype(BF16)
    lo = (c2 - hi.astype(F32) - mid.astype(F32)).astype(BF16)
    kx = jnp.stack([-hi, -mid, -lo], axis=-1)
    kx = jnp.pad(kx, ((0, 0), (0, 0), (0, 0), (0, HEAD_DIM - 3)))
    spec = pl.BlockSpec((1, 1, seq, HEAD_DIM), lambda b, h, i: (b, h, 0, 0))
    return _wide_softmax_attention(o_buf, head_off, True, N_FOX, qkv, HEAD_DIM, _group(q_off),
                                   qkv, _group(k_off), _group(v_off), kx, spec, gain, "fox_attention")


def _xattn_kernel(q_ref, k_ref, v_ref, o_ref):
    for h in range(N_XHEADS):
        sl = slice(h * HEAD_DIM, (h + 1) * HEAD_DIM)
        s = _qk(q_ref[0, :, sl], k_ref[0, :, sl])
        p = jnp.exp(s - jnp.max(s, axis=-1, keepdims=True))
        o = jnp.dot(p.astype(BF16), v_ref[0, :, sl], preferred_element_type=F32)
        o_ref[0, :, sl] = (o / jnp.sum(p, axis=-1, keepdims=True)).astype(o_ref.dtype)


def cross_attention(q, k, v, bq=512):
    nb, seq, w = q.shape
    n_mem = k.shape[1]
    bq = min(bq, seq)
    return pl.pallas_call(
        _xattn_kernel,
        out_shape=jax.ShapeDtypeStruct((nb, seq, w), BF16),
        grid=(nb, seq // bq),
        in_specs=[pl.BlockSpec((1, bq, w), lambda b, i: (b, i, 0)),
                  pl.BlockSpec((1, n_mem, w), lambda b, i: (b, 0, 0)),
                  pl.BlockSpec((1, n_mem, w), lambda b, i: (b, 0, 0))],
        out_specs=pl.BlockSpec((1, bq, w), lambda b, i: (b, i, 0)),
        compiler_params=_params("parallel", "parallel"), name="cross_attention",
    )(q, k, v)


def _t5_bucket(rel):
    n = jnp.maximum(rel, 0)
    max_exact = N_BUCKETS // 2
    nf = jnp.maximum(n, 1).astype(F32)
    large = max_exact + (jnp.log(nf / max_exact) / math.log(MAX_DISTANCE / max_exact)
                         * (N_BUCKETS - max_exact)).astype(jnp.int32)
    large = jnp.minimum(large, N_BUCKETS - 1)
    return jnp.where(n < max_exact, n, large)


def _rope(x, cos, sin):
    x1, x2 = jnp.split(x, 2, axis=-1)
    return jnp.concatenate([x1 * cos - x2 * sin, x2 * cos + x1 * sin], axis=-1)


HW = N_SB * HEAD_DIM
IN_DF = 3 * HW
IN_CQ = 6 * HW
IN_CKV = IN_CQ + MLA_Q_RANK
IN_KR = IN_CKV + MLA_KV_RANK
IN_FX = IN_KR + MLA_ROPE
IN_FL = IN_FX + 3 * HW
SMALL_CKV = 2 * MLA_KV_RANK
SMALL_KR = SMALL_CKV + MLA_KV_RANK
SMALL_FL = SMALL_KR + HEAD_DIM
SMALL_N = SMALL_FL + HEAD_DIM


def _prep_w_in_kernel(w_ref, scale_ref, main_ref, small_ref):
    w = w_ref[...] * scale_ref[...]
    main_ref[:, :IN_CQ] = w[:, :IN_CQ].astype(BF16)
    main_ref[:, IN_CQ:] = w[:, IN_FX:IN_FL].astype(BF16)
    small_ref[...] = jnp.zeros_like(small_ref)
    small_ref[:, :MLA_Q_RANK] = w[:, IN_CQ:IN_CKV].astype(BF16)
    small_ref[:, SMALL_CKV:SMALL_KR] = w[:, IN_CKV:IN_KR].astype(BF16)
    small_ref[:, SMALL_KR:SMALL_KR + MLA_ROPE] = w[:, IN_KR:IN_FX].astype(BF16)
    small_ref[:, SMALL_FL:SMALL_FL + N_FOX] = w[:, IN_FL:IN_FL + N_FOX].astype(BF16)


def _prep_w_in(w_in, bd=128):
    nl, d, n_in = w_in.shape
    sc = HEAD_DIM ** -0.5 * LOG2E
    sc_df = (HEAD_DIM // 2) ** -0.5 * LOG2E
    col_scale = jnp.ones((1, n_in), F32)
    col_scale = col_scale.at[:, 0:HW].set(sc).at[:, IN_DF:IN_DF + HW].set(sc_df).at[:, IN_FX:IN_FX + HW].set(sc)
    w_bf = (jnp.swapaxes(w_in, 1, 2) * col_scale.reshape(n_in, 1)).astype(BF16)
    zeros = lambda n: jnp.zeros((nl, n, d), BF16)
    w_main_t = jnp.concatenate([w_bf[:, :IN_CQ], w_bf[:, IN_FX:IN_FL]], axis=1)
    w_small_t = jnp.concatenate([w_bf[:, IN_CQ:IN_CKV], zeros(SMALL_CKV - MLA_Q_RANK), w_bf[:, IN_CKV:IN_FX],
                                 zeros(SMALL_FL - SMALL_KR - MLA_ROPE), w_bf[:, IN_FL:IN_FL + N_FOX],
                                 zeros(SMALL_N - SMALL_FL - N_FOX)], axis=1)
    return w_main_t, w_small_t


def _mixer(h, nb, seq, cos, sin, t5_table, w_main, w_small, q_norm, w_uq, kv_norm, w_ukv, lam_vecs, b_forget,
           head_norm, w_out, layer_idx):
    sc_mla = (MLA_NOPE + MLA_ROPE) ** -0.5 * LOG2E
    qkv = matmul(h, w_main, BF16, layer=layer_idx, rhs_t=True).reshape(nb, seq, 9 * HW)
    small = matmul(h, w_small, F32, bm=512, bn=SMALL_N, layer=layer_idx, rhs_t=True)
    k_r = small[:, SMALL_KR:SMALL_KR + MLA_ROPE]
    f_logit = small[:, SMALL_FL:SMALL_FL + N_FOX]

    g_sb, g_df, g_mla, g_fx = jnp.split(head_norm, [N_SB, N_SB + N_DIFF, N_SB + N_DIFF + N_MLA], axis=0)
    o = jnp.zeros((nb, seq, (N_SB + N_DIFF + N_MLA + N_FOX) * HEAD_DIM), BF16)

    o = sb_attention(o, 0, qkv, g_sb, 0, N_SB, 2 * N_SB)

    lv = lam_vecs.astype(F32)
    lam_init = 0.8 - 0.6 * math.exp(-0.3 * layer_idx)
    lam = jnp.exp(jnp.sum(lv[0] * lv[1])) - jnp.exp(jnp.sum(lv[2] * lv[3])) + lam_init
    o = diff_attention(o, N_SB, qkv, g_df, t5_table, lam, 1.0 - lam_init, 3 * N_SB, 4 * N_SB, 5 * N_SB)

    wq = (w_uq * sc_mla).reshape(MLA_Q_RANK, N_MLA, MLA_NOPE + MLA_ROPE)
    w_r = wq[..., MLA_NOPE:]
    w_rot = jnp.concatenate([w_r[..., MLA_ROPE // 2:], w_r[..., :MLA_ROPE // 2]], axis=-1)
    wq = jnp.concatenate([wq, w_rot], axis=-1).reshape(MLA_Q_RANK, N_MLA * MLA_QK_PAD).astype(BF16)
    zeros = jnp.zeros_like(cos)
    cs = jnp.concatenate([cos, cos, zeros, zeros], axis=-1).reshape(nb * seq, HEAD_DIM)
    sn = jnp.concatenate([-sin, sin, zeros, zeros], axis=-1).reshape(nb * seq, HEAD_DIM)
    q_cat = norm_matmul(small, 0, MLA_Q_RANK, q_norm, wq, rope=(cs, sn, N_MLA))
    q_cat = q_cat.reshape(nb, seq, N_MLA * MLA_QK_PAD)
    kv = norm_matmul(small, SMALL_CKV // MLA_KV_RANK, MLA_KV_RANK, kv_norm, w_ukv.astype(BF16))
    kv = kv.reshape(nb, seq, N_MLA * 2 * HEAD_DIM)
    k_rope = _rope(k_r.reshape(nb, seq, MLA_ROPE), cos, sin)
    k_rope_pad = jnp.pad(k_rope, ((0, 0), (0, 0), (0, HEAD_DIM - MLA_ROPE))).astype(BF16)
    o = mla_attention(o, N_SB + N_DIFF, q_cat, kv, k_rope_pad, g_mla)

    log_f = jax.nn.log_sigmoid(f_logit.reshape(nb, seq, N_FOX) + b_forget.astype(F32))
    c = lax.cumsum(log_f, axis=1)
    o = fox_attention(o, N_SB + N_DIFF + N_MLA, qkv, g_fx, c, 6 * N_SB, 7 * N_SB, 8 * N_SB)

    return matmul(o.reshape(nb * seq, -1), w_out, F32, layer=layer_idx)


def kernel(x, mem, positions, t5_table, mix_norm_pre, w_in, mla_q_norm, w_uq, mla_kv_norm, w_ukv, diff_lambda, b_forget, head_norm, w_out, mix_norm_post, xattn_norm_pre, mem_norm, w_xq, w_xk, w_xv, w_xo, xattn_norm_post, ffn_norm_pre, w_gate, w_up, conv_w, conv_b, w_down, ffn_norm_post):
    nb, seq, d = x.shape
    depth = w_in.shape[0]
    n_mem = mem.shape[1]
    d_ff = w_gate.shape[2]
    xw = N_XHEADS * HEAD_DIM
    inv = ROPE_THETA ** (-jnp.arange(0, MLA_ROPE, 2, dtype=F32) / MLA_ROPE)
    ang = positions.astype(F32)[..., None] * inv
    cos, sin = jnp.cos(ang), jnp.sin(ang)

    xf = x.reshape(nb * seq, d)
    memf = mem.reshape(nb * n_mem, d)
    w_main, w_small = _prep_w_in(w_in)
    w_out, w_xo = w_out.astype(BF16), w_xo.astype(BF16)
    w_gate, w_up, w_down = w_gate.astype(BF16), w_up.astype(BF16), w_down.astype(BF16)
    h = rmsnorm(xf, mix_norm_pre[0])
    for l in range(depth):
        y = _mixer(h, nb, seq, cos, sin, t5_table, w_main, w_small, mla_q_norm[l], w_uq[l], mla_kv_norm[l],
                   w_ukv[l], diff_lambda[l], b_forget[l], head_norm[l], w_out, l)
        xf, h = addnorm(y, xf, mix_norm_post[l], xattn_norm_pre[l])

        mem_n = rmsnorm(memf, mem_norm[l])
        q = matmul(h, (w_xq[l] * HEAD_DIM ** -0.5).astype(BF16), BF16).reshape(nb, seq, xw)
        k = matmul(mem_n, w_xk[l].astype(BF16), BF16).reshape(nb, n_mem, xw)
        v = matmul(mem_n, w_xv[l].astype(BF16), BF16).reshape(nb, n_mem, xw)
        o = cross_attention(q, k, v).reshape(nb * seq, xw)
        xf, h = matmul_addnorm(o, w_xo, xf, xattn_norm_post[l], ffn_norm_pre[l], layer=l)

        act_main, act_tail = ffn_up(h, w_gate, w_up, conv_w[l], conv_b[l], seq, layer=l)
        y = ffn_down(act_main, act_tail, w_down, l)
        g_next = mix_norm_pre[l + 1] if l + 1 < depth else None
        xf, h = addnorm(y, xf, ffn_norm_post[l], g_next)
    return xf.reshape(nb, seq, d)
```
